```python
import jax, jax.numpy as jnp
from jax import lax
import numpy as np

D_MODEL = 1024
BATCH = 16
SEQ = 256
DEPTH = 2
DEC_BATCH = 2
DEC_SEQ = 2048
PAST_LEN = 512

GRID_W = 64
EXPAND = 2
D_INNER = EXPAND * D_MODEL
D_A = D_INNER // 2
DK_A = 128
H_A = D_A // DK_A
DV_A = D_A // H_A
D_B = D_INNER // 2
HS_B = 64
H_B = D_B // HS_B
R_DECAY = 64
R_ICL = 64
D_C = D_INNER
H_C = 4
DH_C = D_C // H_C
CHUNK_A = 32
CHUNK_C = 64
N_EVEN = (DEPTH + 1) // 2
N_ODD = DEPTH // 2
SHIFT_W_B = 3 * D_B + 2 * R_DECAY + 2 * R_ICL
IN_EVEN = 5 * D_A + SHIFT_W_B + D_B
IN_ODD = 5 * D_C + 4 * H_C
EPS = 1e-6
GN_EPS_B = 64e-5
F32 = jnp.float32

kernel_name = "bidir_hgrn2_rwkv7_mlstm_ctx_prefix_step"


def _split(x, sizes):
    return jnp.split(x, np.cumsum(sizes)[:-1].tolist(), axis=-1)


def _rms_norm(x, g):
    xf = x.astype(F32)
    y = xf * lax.rsqrt(jnp.mean(xf * xf, axis=-1, keepdims=True) + EPS)
    return (y * g.astype(F32)).astype(x.dtype)


def _head_rms(y, g):
    b, t = y.shape[:2]
    y = y * lax.rsqrt(jnp.mean(y * y, axis=-1, keepdims=True) + EPS)
    return y.reshape(b, t, -1) * g.astype(F32)


def _head_group_norm(y, g, bias):
    b, t = y.shape[:2]
    yc = y - jnp.mean(y, axis=-1, keepdims=True)
    yn = yc * lax.rsqrt(jnp.mean(yc * yc, axis=-1, keepdims=True) + GN_EPS_B)
    return yn.reshape(b, t, -1) * g.astype(F32) + bias.astype(F32)


def _ada(cond, w, b):
    m = jax.nn.silu(cond) @ w + b
    shift, scale, gate = jnp.split(m, 3, axis=-1)
    return shift[:, None], scale[:, None], gate[:, None]


def _centred_shift(p, mu):
    prev = jnp.pad(p[:, :-1], ((0, 0), (1, 0), (0, 0)))
    nxt = jnp.pad(p[:, 1:], ((0, 0), (0, 1), (0, 0)))
    return p + mu[0] * (prev - p) + mu[1] * (nxt - p)


def _flip(a):
    return jnp.flip(a, axis=1)


def _to_chunks(a, L):
    b, t = a.shape[:2]
    a = a.reshape((b, t // L, L) + a.shape[2:])
    return jnp.moveaxis(jnp.moveaxis(a, 1, 0), 2, 3)


def _from_chunks(a):
    a = jnp.moveaxis(jnp.moveaxis(a, 3, 2), 0, 1)
    return a.reshape((a.shape[0], a.shape[1] * a.shape[2]) + a.shape[3:])


def _hgrn2_chunked(q, k, v, log_f, s0):
    L = CHUNK_A
    causal = jnp.tril(jnp.ones((L, L), dtype=bool))[:, :, None]

    def step(S, blk):
        qc, kc, vc, gc = blk
        b = jnp.cumsum(gc, axis=2)
        o = jnp.einsum("bhtd,bhde->bhte", qc * jnp.exp(b), S)
        rel = jnp.where(causal, b[:, :, :, None, :] - b[:, :, None, :, :], -jnp.inf)
        att = jnp.einsum("bhtd,bhsd,bhtsd->bhts", qc, kc, jnp.exp(rel))
        o = o + jnp.einsum("bhts,bhse->bhte", att, vc)
        b_end = b[:, :, -1:, :]
        S = jnp.exp(b_end[:, :, 0, :, None]) * S + jnp.einsum("bhsd,bhse->bhde", kc * jnp.exp(b_end - b), vc)
        return S, o

    S, o = lax.scan(step, s0, tuple(_to_chunks(a, L) for a in (q, k, v, log_f)))
    return _from_chunks(o), S


def _rwkv7_scan(r, log_w, k, v, kk, a, s0):
    def step(S, inp):
        rt, lwt, kt, vt, kkt, at = inp
        sa = jnp.einsum("bhvk,bhk->bhv", S, kkt)
        S = (S * jnp.exp(lwt)[:, :, None, :] - sa[..., None] * (kkt * at)[:, :, None, :]
             + vt[..., None] * kt[:, :, None, :])
        return S, jnp.einsum("bhvk,bhk->bhv", S, rt)

    S, y = lax.scan(step, s0, tuple(jnp.moveaxis(t, 1, 0) for t in (r, log_w, k, v, kk, a)))
    return jnp.moveaxis(y, 0, 1), S


def _mlstm_chunked(q, k, v, log_i, log_f, C0, n0, m0):
    L = CHUNK_C
    causal = jnp.tril(jnp.ones((L, L), dtype=bool))

    def step(carry, blk):
        C, n, m = carry
        qc, kc, vc, ic, fc = blk
        b = jnp.cumsum(fc, axis=-1)
        log_d = jnp.where(causal, b[..., :, None] - b[..., None, :] + ic[..., None, :], -jnp.inf)
        log_prev = b + m[..., None]
        m_t = jnp.maximum(log_prev, jnp.max(log_d, axis=-1))
        w_prev = jnp.exp(log_prev - m_t)
        s = jnp.einsum("bhtd,bhsd->bhts", qc, kc) * jnp.exp(log_d - m_t[..., None])
        num = w_prev[..., None] * jnp.einsum("bhtd,bhde->bhte", qc, C) + jnp.einsum("bhts,bhse->bhte", s, vc)
        den = w_prev * jnp.einsum("bhtd,bhd->bht", qc, n) + jnp.sum(s, axis=-1)
        h = num / jnp.maximum(jnp.abs(den), jnp.exp(-m_t))[..., None]
        log_s = b[..., -1:] - b + ic
        m_new = jnp.maximum(b[..., -1] + m, jnp.max(log_s, axis=-1))
        w_s = jnp.exp(log_s - m_new[..., None])
        w_old = jnp.exp(b[..., -1] + m - m_new)
        C = w_old[..., None, None] * C + jnp.einsum("bhs,bhsd,bhse->bhde", w_s, kc, vc)
        n = w_old[..., None] * n + jnp.einsum("bhs,bhsd->bhd", w_s, kc)
        return (C, n, m_new), h

    (C, n, m), h = lax.scan(step, (C0, n0, m0), tuple(_to_chunks(a, L) for a in (q, k, v, log_i, log_f)))
    return _from_chunks(h), C, n, m


def _short_conv(x, w, b, rows):
    bsz, t, ch = x.shape
    w = w.astype(x.dtype)
    if rows is None:
        y = lax.conv_general_dilated(x, w[1][:, None, :], (1,), "SAME",
                                     dimension_numbers=("NWC", "WIO", "NWC"), feature_group_count=ch)
    else:
        y = lax.conv_general_dilated(x.reshape(bsz, rows, GRID_W, ch), w[:, :, None, :], (1, 1), "SAME",
                                     dimension_numbers=("NHWC", "HWIO", "NHWC"),
                                     feature_group_count=ch).reshape(bsz, t, ch)
    return y + b.astype(x.dtype)


def _hgrn_rwkv_mixer(h, w_in, w_out, lb, hg_g, mu, w0, w2, a0, a2, k_k, k_a, r_k, gn_g, gn_b, s_hgrn, s_rwkv):
    bsz, t, _ = h.shape
    q_a, i_a, ff_a, fb_a, z_a, sh_in, z_b = _split(h @ w_in, [D_A] * 5 + [SHIFT_W_B, D_B])
    heads = lambda a, nh: a.astype(F32).reshape(bsz, t, nh, -1)
    lb = lb.astype(F32)
    q = heads(q_a, H_A)
    v_a = heads(i_a, H_A)

    def hgrn_gate(f_pre):
        f = lb + (1.0 - lb) * jax.nn.sigmoid(f_pre.astype(F32))
        return heads(jnp.log(f), H_A), heads(1.0 - f, H_A)

    lf_f, k_f = hgrn_gate(ff_a)
    lf_b, k_b = hgrn_gate(fb_a)
    s_hgrn = s_hgrn.astype(F32)
    o_f, sa_f = _hgrn2_chunked(q, k_f, v_a, lf_f, s_hgrn[:, 0])
    o_b, sa_b = _hgrn2_chunked(_flip(q), _flip(k_b), _flip(v_a), _flip(lf_b), s_hgrn[:, 1])
    out_a = _head_rms(o_f + _flip(o_b), hg_g) * jax.nn.silu(z_a.astype(F32))
    sh = _centred_shift(sh_in.astype(F32), mu.astype(F32))
    r, k, v, wl_f, wl_b, al_f, al_b = _split(sh, [D_B] * 3 + [R_DECAY] * 2 + [R_ICL] * 2)
    kk = heads(k * k_k, H_B)
    kk = kk / jnp.maximum(jnp.sqrt(jnp.sum(kk * kk, axis=-1, keepdims=True)), 1e-12)
    r_h = heads(r, H_B)
    v_h = heads(v, H_B)
    s_rwkv = s_rwkv.astype(F32)

    def rwkv_dir(wl, al, d, rev):
        u = w0[d] + jnp.tanh(wl) @ w2[d]
        log_w = heads(-jnp.exp(-jax.nn.softplus(-u) - 0.5), H_B)
        a = jax.nn.sigmoid(a0[d] + al @ a2[d])
        kd = k * (1.0 + (a - 1.0) * k_a)
        a, kd = heads(a, H_B), heads(kd, H_B)
        bonus = jnp.sum(r_h * kd * r_k, axis=-1, keepdims=True) * v_h
        seq = (r_h, log_w, kd, v_h, kk, a)
        if rev:
            seq = tuple(_flip(x) for x in seq)
        y, s = _rwkv7_scan(*seq, s_rwkv[:, d])
        return (_flip(y) if rev else y), bonus, s

    y_f, bo_f, sr_f = rwkv_dir(wl_f, al_f, 0, False)
    y_b, bo_b, sr_b = rwkv_dir(wl_b, al_b, 1, True)
    out_b = (_head_group_norm(y_f + y_b, gn_g, gn_b) + (bo_f + bo_b).reshape(bsz, t, -1)) * jax.nn.silu(z_b.astype(F32))
    out = jnp.concatenate([out_a, out_b], axis=-1).astype(h.dtype) @ w_out
    return out, jnp.stack([sa_f, sa_b], axis=1), jnp.stack([sr_f, sr_b], axis=1)


def _mlstm_mixer(h, w_in, w_out, conv_w, conv_b, gate_b, norm_g, s_c, s_n, s_m, rows):
    bsz, t, _ = h.shape
    qk, v, o, z, gates = _split(h @ w_in, [2 * D_C, D_C, D_C, D_C, 4 * H_C])
    qk = jax.nn.silu(_short_conv(qk, conv_w, conv_b, rows)).astype(F32)
    heads = lambda a: a.astype(F32).reshape(bsz, t, H_C, DH_C)
    q = heads(qk[..., :D_C])
    k = heads(qk[..., D_C:]) * (DH_C ** -0.5)
    v = heads(v)
    g = gates.astype(F32).reshape(bsz, t, 4, H_C) + gate_b.astype(F32)
    s_c, s_n, s_m = s_c.astype(F32), s_n.astype(F32), s_m.astype(F32)

    def run(d, rev):
        seq = (q, k, v, g[:, :, d], jax.nn.log_sigmoid(g[:, :, 2 + d]))
        if rev:
            seq = tuple(_flip(a) for a in seq)
        hd, C, n, m = _mlstm_chunked(*seq, s_c[:, d], s_n[:, d], s_m[:, d])
        return (_flip(hd) if rev else hd), C, n, m

    h_f, c_f, n_f, m_f = run(0, False)
    h_b, c_b, n_b, m_b = run(1, True)
    y = jax.nn.sigmoid(heads(o)) * (h_f + h_b)
    y = _head_rms(y, norm_g) * jax.nn.silu(z.astype(F32))
    return (y.astype(h.dtype) @ w_out, jnp.stack([c_f, c_b], axis=1),
            jnp.stack([n_f, n_b], axis=1), jnp.stack([m_f, m_b], axis=1))


def setup_inputs(seed: int = 0) -> dict:
    key = jax.random.key(seed)
    ks = iter(jax.random.split(key, 48))
    nrm = lambda shape, s: jax.random.normal(next(ks), shape, F32) * s
    inp = {}
    inp["x_prompt"] = nrm((BATCH, SEQ, D_MODEL), 1.0)
    inp["x_sample"] = nrm((DEC_BATCH, DEC_SEQ, D_MODEL), 1.0)
    inp["c"] = nrm((DEC_BATCH, D_MODEL), 1.0)
    inp["state_hgrn"] = nrm((DEC_BATCH, N_EVEN, 2, H_A, DK_A, DV_A), 0.3)
    inp["state_rwkv"] = nrm((DEC_BATCH, N_EVEN, 2, H_B, HS_B, HS_B), 0.1)
    inp["state_mlstm_C"] = nrm((DEC_BATCH, N_ODD, 2, H_C, DH_C, DH_C), 0.05)
    inp["state_mlstm_n"] = nrm((DEC_BATCH, N_ODD, 2, H_C, DH_C), 0.1)
    inp["state_mlstm_m"] = nrm((DEC_BATCH, N_ODD, 2, H_C), 0.5)
    inp["c_ctx"] = nrm((D_MODEL,), 1.0)
    inp["w_mod"] = nrm((DEPTH, D_MODEL, 3 * D_MODEL), D_MODEL ** -0.5)
    inp["b_mod"] = nrm((DEPTH, 3 * D_MODEL), 0.02)
    inp["norm_g"] = 1.0 + nrm((DEPTH, D_MODEL), 0.02)
    inp["final_norm_g"] = 1.0 + nrm((D_MODEL,), 0.02)
    inp["w_in_even"] = nrm((N_EVEN, D_MODEL, IN_EVEN), D_MODEL ** -0.5)
    inp["w_out_even"] = nrm((N_EVEN, D_INNER, D_MODEL), D_INNER ** -0.5)
    inp["hgrn_lb_logits"] = nrm((N_EVEN + 1, D_A), 0.5)
    inp["hgrn_norm_g"] = 1.0 + nrm((N_EVEN, D_A), 0.02)
    inp["rwkv_shift_mu"] = jax.random.uniform(next(ks), (N_EVEN, 2, SHIFT_W_B), F32, 0.0, 0.5)
    inp["rwkv_w0"] = nrm((N_EVEN, 2, D_B), 0.5)
    inp["rwkv_w2"] = nrm((N_EVEN, 2, R_DECAY, D_B), 0.5 * R_DECAY ** -0.5)
    inp["rwkv_a0"] = nrm((N_EVEN, 2, D_B), 0.1)
    inp["rwkv_a2"] = nrm((N_EVEN, 2, R_ICL, D_B), 0.5 * R_ICL ** -0.5)
    inp["rwkv_k_k"] = 0.85 + nrm((N_EVEN, D_B), 0.05)
    inp["rwkv_k_a"] = 1.0 + nrm((N_EVEN, D_B), 0.05)
    inp["rwkv_r_k"] = nrm((N_EVEN, H_B, HS_B), 0.1)
    inp["rwkv_gn_g"] = 1.0 + nrm((N_EVEN, D_B), 0.02)
    inp["rwkv_gn_b"] = nrm((N_EVEN, D_B), 0.02)
    inp["w_in_odd"] = nrm((N_ODD, D_MODEL, IN_ODD), D_MODEL ** -0.5)
    inp["w_out_odd"] = nrm((N_ODD, D_C, D_MODEL), D_C ** -0.5)
    inp["mlstm_conv_w"] = nrm((N_ODD, 3, 3, 2 * D_C), 1.0 / 3.0)
    inp["mlstm_conv_b"] = nrm((N_ODD, 2 * D_C), 0.02)
    inp["mlstm_gate_b"] = nrm((N_ODD, 4, H_C), 0.1) + jnp.array([0.0, 0.0, 3.0, 3.0], F32)[None, :, None]
    inp["mlstm_norm_g"] = 1.0 + nrm((N_ODD, D_C), 0.02)
    return inp


def reference(x_prompt, x_sample, c, state_hgrn, state_rwkv, state_mlstm_C, state_mlstm_n, state_mlstm_m,
              c_ctx, w_mod, b_mod, norm_g, final_norm_g, w_in_even, w_out_even, hgrn_lb_logits, hgrn_norm_g,
              rwkv_shift_mu, rwkv_w0, rwkv_w2, rwkv_a0, rwkv_a2, rwkv_k_k, rwkv_k_a, rwkv_r_k, rwkv_gn_g,
              rwkv_gn_b, w_in_odd, w_out_odd, mlstm_conv_w, mlstm_conv_b, mlstm_gate_b, mlstm_norm_g):
    rows = x_sample.shape[1] // GRID_W
    n_p = x_prompt.shape[0]
    lb_all = jnp.cumsum(jax.nn.softmax(hgrn_lb_logits.astype(F32), axis=0), axis=0)
    xp, xs = x_prompt, x_sample
    new_hgrn, new_rwkv, new_c, new_n, new_m = [], [], [], [], []
    for layer in range(DEPTH):
        j = layer // 2
        sh_p, sc_p, g_p = _ada(c_ctx[None], w_mod[layer], b_mod[layer])
        sh_s, sc_s, g_s = _ada(c, w_mod[layer], b_mod[layer])
        h_p = _rms_norm(xp, norm_g[layer]) * (1.0 + sc_p) + sh_p
        h_s = _rms_norm(xs, norm_g[layer]) * (1.0 + sc_s) + sh_s
        if layer % 2 == 0:
            p = (w_in_even[j], w_out_even[j], lb_all[j], hgrn_norm_g[j], rwkv_shift_mu[j], rwkv_w0[j], rwkv_w2[j],
                 rwkv_a0[j], rwkv_a2[j], rwkv_k_k[j], rwkv_k_a[j], rwkv_r_k[j], rwkv_gn_g[j], rwkv_gn_b[j])
            o_p, s_h, s_r = _hgrn_rwkv_mixer(h_p, *p, jnp.zeros((n_p, 2, H_A, DK_A, DV_A), F32),
                                             jnp.zeros((n_p, 2, H_B, HS_B, HS_B), F32))
            o_s, _, _ = _hgrn_rwkv_mixer(h_s, *p, state_hgrn[:, j], state_rwkv[:, j])
            new_hgrn.append(s_h)
            new_rwkv.append(s_r)
        else:
            p = (w_in_odd[j], w_out_odd[j], mlstm_conv_w[j], mlstm_conv_b[j], mlstm_gate_b[j], mlstm_norm_g[j])
            o_p, s_c, s_n, s_m = _mlstm_mixer(h_p, *p, jnp.zeros((n_p, 2, H_C, DH_C, DH_C), F32),
                                              jnp.zeros((n_p, 2, H_C, DH_C), F32),
                                              jnp.zeros((n_p, 2, H_C), F32), None)
            o_s, _, _, _ = _mlstm_mixer(h_s, *p, state_mlstm_C[:, j], state_mlstm_n[:, j], state_mlstm_m[:, j], rows)
            new_c.append(s_c)
            new_n.append(s_n)
            new_m.append(s_m)
        xp = xp + g_p * o_p
        xs = xs + g_s * o_s
    y_prompt = _rms_norm(xp, final_norm_g)
    y_sample = _rms_norm(xs, final_norm_g)
    dt = x_prompt.dtype
    new_hgrn = jnp.stack(new_hgrn, axis=1).astype(dt)
    new_rwkv = jnp.stack(new_rwkv, axis=1).astype(dt)
    new_mlstm_C = jnp.stack(new_c, axis=1).astype(dt)
    new_mlstm_n = jnp.stack(new_n, axis=1).astype(dt)
    new_mlstm_m = jnp.stack(new_m, axis=1).astype(dt)
    return (y_prompt, y_sample, new_hgrn, new_rwkv, new_mlstm_C, new_mlstm_n, new_mlstm_m)
```

```python
import functools

import numpy as np
import jax
import jax.numpy as jnp
from jax import lax
from jax.experimental import pallas as pl
from jax.experimental.pallas import tpu as pltpu

F32 = jnp.float32
BF16 = jnp.bfloat16

D_MODEL = 1024
BATCH, SEQ = 16, 256
DEC_BATCH, DEC_SEQ = 2, 2048
GRID_W = 64
N_P = BATCH * SEQ
N_S = DEC_BATCH * DEC_SEQ
N_TOK = N_P + N_S
D_A, DK_A, H_A = 1024, 128, 8
D_B, HS_B, H_B = 1024, 64, 16
R_LOW = 64
D_C, H_C, DH_C = 2048, 4, 512
SHIFT_W_B = 3 * D_B + 4 * R_LOW
IN_EVEN = 5 * D_A + SHIFT_W_B + D_B
IN_EVEN_PAD = 9728
IN_ODD_MAIN = 5 * D_C
EPS = 1e-6
GN_EPS_B = 64e-5

ROW_BLK = 256
L_HGRN = 64
L_RWKV = 32
L_MLSTM = 256
LANE = 128

NN = (((1,), (0,)), ((), ()))
NT = (((1,), (1,)), ((), ()))
TN = (((0,), (0,)), ((), ()))


def _dot(a, b, dims=NN):
    return lax.dot_general(a.astype(BF16), b.astype(BF16), dims, preferred_element_type=F32)


def _split2(x):
    hi = x.astype(BF16)
    lo = (x - hi.astype(F32)).astype(BF16)
    return hi, lo


def _dot_sel(g, x):
    hi, lo = _split2(x)
    return (lax.dot_general(g, hi, NN, preferred_element_type=F32)
            + lax.dot_general(g, lo, NN, preferred_element_type=F32))


def _dot_sel_r(x, g):
    hi, lo = _split2(x)
    return (lax.dot_general(hi, g, NN, preferred_element_type=F32)
            + lax.dot_general(lo, g, NN, preferred_element_type=F32))


def _sigmoid(x):
    return 1.0 / (1.0 + jnp.exp(-x))


def _silu(x):
    return x * _sigmoid(x)


def _logsig(x):
    return jnp.minimum(x, 0.0) - jnp.log(1.0 + jnp.exp(-jnp.abs(x)))


def _hgrn_consts(L):
    nlev = int(np.log2(L))
    t = np.arange(L)
    G = np.zeros((nlev + 2, L, L), np.float32)
    G[0] = t[None, :] <= t[:, None]
    G[1] = t[None, :] > t[:, None]
    M = np.zeros((nlev + 1, L, L), np.float32)
    M[0] = np.eye(L)
    for lev in range(nlev):
        m = 1 << lev
        grp, pos = t // (2 * m), t % (2 * m)
        anchor = grp * 2 * m + m - 1
        for r in range(L):
            if pos[r] >= m:
                G[2 + lev, r, anchor[r] + 1:r + 1] = 1.0
            else:
                G[2 + lev, r, r + 1:anchor[r] + 1] = 1.0
        M[1 + lev] = (grp[:, None] == grp[None, :]) & (pos[:, None] >= m) & (pos[None, :] < m)
    G2 = np.stack([G.reshape(-1, L), G[:, ::-1, ::-1].reshape(-1, L)])
    M2 = np.stack([M, M[:, ::-1, ::-1]])
    return jnp.asarray(G2, BF16), jnp.asarray(M2, F32)


def _tri_consts(L):
    t = np.arange(L)
    return np.stack([t[None, :] <= t[:, None], t[None, :] >= t[:, None]]).astype(np.float32)


def _head_sum_const():
    i = np.arange(LANE)
    return jnp.asarray((i[:, None] // HS_B) == (i[None, :] // HS_B), BF16)


def _hgrn_chunk(q, v, pre, lb, St, G, M_ref, d):
    L = q.shape[0]
    nlev = M_ref.shape[1] - 1
    f = lb + (1.0 - lb) * _sigmoid(pre)
    logf = jnp.log(f)
    k = 1.0 - f
    E = _dot_sel(G, logf)
    binc, rem = E[0:L], E[L:2 * L]
    tot = jnp.sum(logf, axis=0, keepdims=True)
    o = _dot(q * jnp.exp(binc), St, NT)
    att = M_ref[d, 0] * _dot(q, k, NT)
    for lev in range(nlev):
        F = jnp.exp(E[(2 + lev) * L:(3 + lev) * L])
        att = att + M_ref[d, 1 + lev] * _dot(q * F, k * F, NT)
    o = o + _dot(att, v)
    St = jnp.exp(tot) * St + _dot(v, k * jnp.exp(rem), TN)
    return o, St


def _rwkv_chunk(r, lw, kd, v, kk, bv, S, tri, strict, incl):
    L, K = r.shape
    ginc = _dot_sel(tri, lw)
    gtot = jnp.sum(lw, axis=0, keepdims=True)
    e_inc = jnp.exp(ginc)
    e_exc = jnp.exp(ginc - lw)
    e_neg = jnp.exp(-ginc)
    e_end = jnp.exp(gtot - ginc)
    Kq, Rq = kk * e_exc, r * e_inc
    Bi, Ki = bv * e_neg, kd * e_neg
    Be, Ke = bv * e_end, kd * e_end
    Mb = strict * _dot(Kq, Bi, NT)
    Mk = strict * _dot(Kq, Ki, NT)
    Nb = incl * _dot(Rq, Bi, NT)
    Nk = incl * _dot(Rq, Ki, NT)
    Z = jnp.concatenate([Kq, _dot(Mk, v)], axis=1)
    X = Z - _dot(Mb, Z)
    Mp = Mb
    n = 2
    while n < L:
        Mp = _dot(Mp, Mp)
        X = X + _dot(Mp, X)
        n *= 2
    A1, A2 = X[:, :K], X[:, K:]
    Qe = Rq - _dot(Nb, A1)
    y = _dot(Qe, S, NT) + _dot(Nk, v) - _dot(Nb, A2)
    Tm = _dot(A1, Be, TN)
    U = _dot(v, Ke, TN) - _dot(A2, Be, TN)
    S = S * jnp.exp(gtot) - _dot(S, Tm) + U
    return y, S


def _mlstm_chunk(q, k, v, gi_col, gf_col, gi_row, gf_row, C, n, m, tri, tri_t, causal):
    L = q.shape[0]
    lf_col = _logsig(gf_col)
    lf_row = _logsig(gf_row)
    b_col = _dot_sel(tri, jnp.broadcast_to(lf_col, (L, LANE)))[:, 0:1]
    b_row = _dot_sel_r(jnp.broadcast_to(lf_row, (8, L)), tri_t)[0:1, :]
    btot = jnp.sum(lf_col, axis=0, keepdims=True)
    log_d = jnp.where(causal > 0, b_col - b_row + gi_row, -jnp.inf)
    m_loc = jnp.max(log_d, axis=1, keepdims=True)
    log_prev = b_col + m
    m_t = jnp.maximum(log_prev, m_loc)
    w_prev = jnp.exp(log_prev - m_t)
    s = _dot(q, k, NT) * jnp.exp(log_d - m_t)
    num = w_prev * _dot(q, C) + _dot(s, v)
    den = w_prev * jnp.sum(q * n, axis=1, keepdims=True) + jnp.sum(s, axis=1, keepdims=True)
    h = num / jnp.maximum(jnp.abs(den), jnp.exp(-m_t))
    log_s = btot - b_col + gi_col
    m_new = jnp.maximum(btot + m, jnp.max(log_s, axis=0, keepdims=True))
    kw = k * jnp.exp(log_s - m_new)
    w_old = jnp.exp(btot + m - m_new)
    C = w_old * C + _dot(kw, v, TN)
    n = w_old * n + jnp.sum(kw, axis=0, keepdims=True)
    return h, C, n, m_new


def _adaln_kernel(c_ref, w_ref, b_ref, o_ref):
    o_ref[0] = _dot(_silu(c_ref[...]), w_ref[0]) + b_ref[0]


def _adaln(cond8, w_mod, b_mod):
    depth, _, n3 = w_mod.shape
    tn = 1024
    return pl.pallas_call(
        _adaln_kernel,
        grid=(depth, n3 // tn),
        in_specs=[pl.BlockSpec((8, D_MODEL), lambda l, j: (0, 0)),
                  pl.BlockSpec((1, D_MODEL, tn), lambda l, j: (l, 0, j)),
                  pl.BlockSpec((1, 1, tn), lambda l, j: (l, 0, j))],
        out_specs=pl.BlockSpec((1, 8, tn), lambda l, j: (l, 0, j)),
        out_shape=jax.ShapeDtypeStruct((depth, 8, n3), F32),
        name="adaln",
    )(cond8, w_mod, b_mod.reshape(depth, 1, n3))


def _rms_mod(x, g, mod):
    y = x * lax.rsqrt(jnp.mean(x * x, axis=-1, keepdims=True) + EPS) * g
    return y * (1.0 + mod[:, D_MODEL:2 * D_MODEL]) + mod[:, 0:D_MODEL]


def _norm_mod_kernel(x_ref, g_ref, mod_ref, h_ref):
    h_ref[...] = _rms_mod(x_ref[...], g_ref[...], mod_ref[0]).astype(BF16)


def _norm_mod(x, g, mod_rows):
    nblk = x.shape[0] // ROW_BLK
    return pl.pallas_call(
        _norm_mod_kernel,
        grid=(nblk,),
        in_specs=[pl.BlockSpec((ROW_BLK, D_MODEL), lambda i: (i, 0)),
                  pl.BlockSpec((1, D_MODEL), lambda i: (0, 0)),
                  pl.BlockSpec((1, 1, 3 * D_MODEL), lambda i: (i, 0, 0))],
        out_specs=pl.BlockSpec((ROW_BLK, D_MODEL), lambda i: (i, 0)),
        out_shape=jax.ShapeDtypeStruct(x.shape, BF16),
        name="norm_mod",
    )(x, g, mod_rows)


def _matmul_kernel(a_ref, w_ref, o_ref):
    o_ref[...] = jnp.dot(a_ref[...], w_ref[...], preferred_element_type=F32)


def _matmul(a, w, tm=1024, tn=512):
    m, k = a.shape
    n = w.shape[1]
    tn = min(tn, n)
    return pl.pallas_call(
        _matmul_kernel,
        grid=(m // tm, n // tn),
        in_specs=[pl.BlockSpec((tm, k), lambda i, j: (i, 0)),
                  pl.BlockSpec((k, tn), lambda i, j: (0, j))],
        out_specs=pl.BlockSpec((tm, tn), lambda i, j: (i, j)),
        out_shape=jax.ShapeDtypeStruct((m, n), F32),
        name="in_proj",
    )(a, w)


def _hgrn_kernel(T, has_state, want_final, *refs):
    q_ref, v_ref, ff_ref, fb_ref, z_ref, lb_ref, gn_ref, G_ref, M_ref = refs[:9]
    pos = 9
    s0_ref = None
    if has_state:
        s0_ref = refs[pos]
        pos += 1
    pos += 1
    out_ref = refs[pos]
    pos += 1
    sfin_ref = None
    if want_final:
        sfin_ref = refs[pos]
        pos += 1
    o_acc = refs[pos]
    L = L_HGRN
    nc = T // L
    lb = lb_ref[0]
    for d in (0, 1):
        f_ref = ff_ref if d == 0 else fb_ref
        G = G_ref[d]

        def step(c, St, d=d, f_ref=f_ref, G=G):
            ci = c if d == 0 else nc - 1 - c
            rows = pl.ds(pl.multiple_of(ci * L, L), L)
            o, St = _hgrn_chunk(q_ref[rows, :], v_ref[rows, :], f_ref[rows, :], lb, St, G, M_ref, d)
            if d == 0:
                o_acc[rows, :] = o
            else:
                o_acc[rows, :] += o
            return St

        St0 = s0_ref[0, d, 0] if has_state else jnp.zeros((DK_A, DK_A), F32)
        St = lax.fori_loop(0, nc, step, St0)
        if want_final:
            sfin_ref[0, d, 0] = St
    o = o_acc[...]
    y = o * lax.rsqrt(jnp.mean(o * o, axis=-1, keepdims=True) + EPS) * gn_ref[0]
    out_ref[...] = (y * _silu(z_ref[...])).astype(BF16)


def _hgrn(proj, lb, gn, consts, prev_out, nb, T, row0, s0_t=None, want_final=False):
    G, M = consts
    rb0 = row0 // T
    col = lambda off: pl.BlockSpec((T, LANE), lambda b, h, off=off: (rb0 + b, off + h))
    in_specs = [col(0), col(8), col(16), col(24), col(32),
                pl.BlockSpec((1, 1, LANE), lambda b, h: (h, 0, 0)),
                pl.BlockSpec((1, 1, LANE), lambda b, h: (h, 0, 0)),
                pl.BlockSpec(G.shape, lambda b, h: (0, 0, 0)),
                pl.BlockSpec(M.shape, lambda b, h: (0, 0, 0, 0))]
    args = [proj, proj, proj, proj, proj, lb, gn, G, M]
    if s0_t is not None:
        in_specs.append(pl.BlockSpec((1, 2, 1, DK_A, DK_A), lambda b, h: (b, 0, h, 0, 0)))
        args.append(s0_t)
    in_specs.append(pl.BlockSpec(memory_space=pl.ANY))
    args.append(prev_out)
    out_specs = [pl.BlockSpec((T, LANE), lambda b, h: (rb0 + b, h))]
    out_shape = [jax.ShapeDtypeStruct(prev_out.shape, BF16)]
    if want_final:
        out_specs.append(pl.BlockSpec((1, 2, 1, DK_A, DK_A), lambda b, h: (b, 0, h, 0, 0)))
        out_shape.append(jax.ShapeDtypeStruct((nb, 2, H_A, DK_A, DK_A), F32))
    res = pl.pallas_call(
        functools.partial(_hgrn_kernel, T, s0_t is not None, want_final),
        grid=(nb, H_A),
        in_specs=in_specs,
        out_specs=out_specs,
        out_shape=out_shape,
        scratch_shapes=[pltpu.VMEM((T, LANE), F32)],
        input_output_aliases={len(args) - 1: 0},
        name="hgrn2_T%d" % T,
    )(*args)
    return res


def _shift_kernel(p_ref, mu_ref, prev_ref, o_ref):
    p = p_ref[...]
    T = p.shape[0]
    t = lax.broadcasted_iota(jnp.int32, p.shape, 0)
    prev = jnp.where(t == 0, 0.0, pltpu.roll(p, 1, 0))
    nxt = jnp.where(t == T - 1, 0.0, pltpu.roll(p, T - 1, 0))
    mu = mu_ref[...]
    o_ref[...] = p + mu[0:1] * (prev - p) + mu[1:2] * (nxt - p)


def _shift(proj, mu, prev_out, nb, T, row0):
    tc = 256
    rb0 = row0 // T
    c0 = (5 * D_A) // tc
    return pl.pallas_call(
        _shift_kernel,
        grid=(nb, SHIFT_W_B // tc),
        in_specs=[pl.BlockSpec((T, tc), lambda b, j: (rb0 + b, c0 + j)),
                  pl.BlockSpec((2, tc), lambda b, j: (0, j)),
                  pl.BlockSpec(memory_space=pl.ANY)],
        out_specs=pl.BlockSpec((T, tc), lambda b, j: (rb0 + b, j)),
        out_shape=jax.ShapeDtypeStruct(prev_out.shape, F32),
        input_output_aliases={2: 0},
        name="rwkv_shift_T%d" % T,
    )(proj, mu, prev_out)


def _rwkv_prep_kernel(r_ref, k_ref, v_ref, low_ref, w0_ref, w2_ref, a0_ref, a2_ref, kk_ref_p, ka_ref, rk_ref,
                      bd_ref, kk_ref, lw_ref, kd_ref, bv_ref, bonus_ref):
    r, k, v, low = r_ref[...], k_ref[...], v_ref[...], low_ref[...]
    bd = bd_ref[...]
    tl = jnp.tanh(low)
    kkr = k * kk_ref_p[...]
    kk = kkr / jnp.maximum(jnp.sqrt(_dot_sel_r(kkr * kkr, bd)), 1e-12)
    kk_ref[...] = kk
    bonus = jnp.zeros_like(r)
    for d in (0, 1):
        u = w0_ref[d] + _dot(tl, w2_ref[d])
        lw_ref[d] = -np.float32(np.exp(-0.5)) * _sigmoid(u)
        a = _sigmoid(a0_ref[d] + _dot(low, a2_ref[d]))
        kd = k * (1.0 + (a - 1.0) * ka_ref[...])
        kd_ref[d] = kd
        bv_ref[d] = a * kk
        bonus = bonus + _dot_sel_r(r * kd * rk_ref[...], bd) * v
    bonus_ref[...] = bonus


def _rwkv_prep(sh, w0, w2p, a0, a2p, k_k, k_a, r_k, bd, tm=1024):
    n = sh.shape[0]
    tile = lambda off: pl.BlockSpec((tm, LANE), lambda i, j, off=off: (i, off + j))
    vec = pl.BlockSpec((1, LANE), lambda i, j: (0, j))
    dvec = pl.BlockSpec((2, 1, LANE), lambda i, j: (0, 0, j))
    dmat = pl.BlockSpec((2, 4 * R_LOW, LANE), lambda i, j: (0, 0, j))
    dout = pl.BlockSpec((2, tm, LANE), lambda i, j: (0, i, j))
    one = jax.ShapeDtypeStruct((n, D_B), F32)
    two = jax.ShapeDtypeStruct((2, n, D_B), F32)
    return pl.pallas_call(
        _rwkv_prep_kernel,
        grid=(n // tm, D_B // LANE),
        in_specs=[tile(0), tile(8), tile(16),
                  pl.BlockSpec((tm, 4 * R_LOW), lambda i, j: (i, (3 * D_B) // (4 * R_LOW))),
                  dvec, dmat, dvec, dmat, vec, vec, vec,
                  pl.BlockSpec((LANE, LANE), lambda i, j: (0, 0))],
        out_specs=[tile(0), dout, dout, dout, tile(0)],
        out_shape=[one, two, two, two, one],
        name="rwkv_prep",
    )(sh, sh, sh, sh, w0, w2p, a0, a2p, k_k, k_a, r_k, bd)


def _rwkv_kernel(T, has_state, want_final, *refs):
    r_ref, v_ref, kk_ref, lw_ref, kd_ref, bv_ref, tri_ref, strict_ref, incl_ref = refs[:9]
    pos = 9
    s0_ref = None
    if has_state:
        s0_ref = refs[pos]
        pos += 1
    pos += 1
    y_ref = refs[pos]
    pos += 1
    sfin_ref = refs[pos] if want_final else None
    L = L_RWKV
    nc = T // L
    d = pl.program_id(1)
    tri = tri_ref[0]
    strict = strict_ref[0]
    incl = incl_ref[0]

    def step(c, carry):
        ci = jnp.where(d == 0, c, nc - 1 - c)
        rows = pl.ds(pl.multiple_of(ci * L, L), L)
        r, v, kk = r_ref[rows, :], v_ref[rows, :], kk_ref[rows, :]
        lw, kd, bv = lw_ref[0, rows, :], kd_ref[0, rows, :], bv_ref[0, rows, :]
        ys, new = [], []
        for hh in (0, 1):
            sl = slice(hh * HS_B, (hh + 1) * HS_B)
            y, S = _rwkv_chunk(r[:, sl], lw[:, sl], kd[:, sl], v[:, sl], kk[:, sl], bv[:, sl], carry[hh],
                               tri, strict, incl)
            ys.append(y)
            new.append(S)
        y_ref[0, rows, :] = jnp.concatenate(ys, axis=1)
        return tuple(new)

    if has_state:
        init = (s0_ref[0, 0, 0], s0_ref[0, 0, 1])
    else:
        init = (jnp.zeros((HS_B, HS_B), F32), jnp.zeros((HS_B, HS_B), F32))
    S0, S1 = lax.fori_loop(0, nc, step, init)
    if want_final:
        sfin_ref[0, 0, 0] = S0
        sfin_ref[0, 0, 1] = S1


def _rwkv(sh, kk, lw, kd, bv, consts, prev_y, nb, T, row0, s0=None, want_final=False):
    tri, strict, incl = consts
    L = L_RWKV
    rb0 = row0 // T
    nhp = H_B // 2
    tok = lambda off: pl.BlockSpec((T, LANE), lambda b, d, p, off=off: (rb0 + b, off + p))
    dtok = pl.BlockSpec((1, T, LANE), lambda b, d, p: (d, rb0 + b, p))
    cst = pl.BlockSpec((1, L, L), lambda b, d, p: (d, 0, 0))
    st = pl.BlockSpec((1, 1, 2, HS_B, HS_B), lambda b, d, p: (b, d, p, 0, 0))
    in_specs = [tok(0), tok(2 * D_B // LANE), tok(0), dtok, dtok, dtok, cst, cst, cst]
    args = [sh, sh, kk, lw, kd, bv, tri, strict, incl]
    if s0 is not None:
        in_specs.append(st)
        args.append(s0)
    in_specs.append(pl.BlockSpec(memory_space=pl.ANY))
    args.append(prev_y)
    out_specs = [dtok]
    out_shape = [jax.ShapeDtypeStruct(prev_y.shape, F32)]
    if want_final:
        out_specs.append(st)
        out_shape.append(jax.ShapeDtypeStruct((nb, 2, H_B, HS_B, HS_B), F32))
    return pl.pallas_call(
        functools.partial(_rwkv_kernel, T, s0 is not None, want_final),
        grid=(nb, 2, nhp),
        in_specs=in_specs,
        out_specs=out_specs,
        out_shape=out_shape,
        input_output_aliases={len(args) - 1: 0},
        name="rwkv7_T%d" % T,
    )(*args)


def _rwkv_combine_kernel(y_ref, bonus_ref, z_ref, g_ref, b_ref, bd_ref, o_ref):
    bd = bd_ref[...]
    y = y_ref[0] + y_ref[1]
    yc = y - _dot_sel_r(y, bd) * (1.0 / HS_B)
    var = _dot_sel_r(yc * yc, bd) * (1.0 / HS_B)
    yn = yc * lax.rsqrt(var + GN_EPS_B) * g_ref[...] + b_ref[...]
    o_ref[...] = ((yn + bonus_ref[...]) * _silu(z_ref[...])).astype(BF16)


def _rwkv_combine(y2, bonus, proj, gn_g, gn_b, bd, tm=1024):
    n = bonus.shape[0]
    zc0 = (5 * D_A + SHIFT_W_B) // LANE
    tile = pl.BlockSpec((tm, LANE), lambda i, j: (i, j))
    vec = pl.BlockSpec((1, LANE), lambda i, j: (0, j))
    return pl.pallas_call(
        _rwkv_combine_kernel,
        grid=(n // tm, D_B // LANE),
        in_specs=[pl.BlockSpec((2, tm, LANE), lambda i, j: (0, i, j)), tile,
                  pl.BlockSpec((tm, LANE), lambda i, j: (i, zc0 + j)), vec, vec,
                  pl.BlockSpec((LANE, LANE), lambda i, j: (0, 0))],
        out_specs=tile,
        out_shape=jax.ShapeDtypeStruct((n, D_B), BF16),
        name="rwkv_combine",
    )(y2, bonus, proj, gn_g, gn_b, bd)


def _out_proj_kernel(final, a_ref, b_ref, w_ref, x_ref, mod_ref, g_ref, *rest):
    acc = jnp.dot(a_ref[...], w_ref[0:D_MODEL, :], preferred_element_type=F32)
    acc = acc + jnp.dot(b_ref[...], w_ref[D_MODEL:2 * D_MODEL, :], preferred_element_type=F32)
    x = x_ref[...] + mod_ref[0][:, 2 * D_MODEL:3 * D_MODEL] * acc
    if final:
        (y_ref,) = rest
        y_ref[...] = x * lax.rsqrt(jnp.mean(x * x, axis=-1, keepdims=True) + EPS) * g_ref[...]
    else:
        nmod_ref, x_out_ref, h_ref = rest
        x_out_ref[...] = x
        h_ref[...] = _rms_mod(x, g_ref[...], nmod_ref[0]).astype(BF16)


def _out_proj(a, b, a_col, b_col, w, x, mod_rows, g, next_mod_rows=None):
    n = x.shape[0]
    nblk = n // ROW_BLK
    final = next_mod_rows is None
    row = lambda c: pl.BlockSpec((ROW_BLK, D_MODEL), lambda i, c=c: (i, c))
    modspec = pl.BlockSpec((1, 1, 3 * D_MODEL), lambda i: (i, 0, 0))
    in_specs = [row(a_col), row(b_col), pl.BlockSpec(w.shape, lambda i: (0, 0)), row(0), modspec,
                pl.BlockSpec((1, D_MODEL), lambda i: (0, 0))]
    args = [a, b, w, x, mod_rows, g]
    if final:
        out_specs = row(0)
        out_shape = jax.ShapeDtypeStruct((n, D_MODEL), F32)
    else:
        in_specs.append(modspec)
        args.append(next_mod_rows)
        out_specs = [row(0), row(0)]
        out_shape = [jax.ShapeDtypeStruct((n, D_MODEL), F32), jax.ShapeDtypeStruct((n, D_MODEL), BF16)]
    return pl.pallas_call(
        functools.partial(_out_proj_kernel, final),
        grid=(nblk,),
        in_specs=in_specs,
        out_specs=out_specs,
        out_shape=out_shape,
        name="out_proj_final" if final else "out_proj",
    )(*args)


def _conv_kernel(two_d, x_ref, w_ref, b_ref, prev_ref, o_ref):
    x = x_ref[...]
    T = x.shape[0]
    t = lax.broadcasted_iota(jnp.int32, x.shape, 0)
    acc = jnp.zeros_like(x) + b_ref[...]
    if two_d:
        rows = T // GRID_W
        r, c = jnp.right_shift(t, int(np.log2(GRID_W))), jnp.bitwise_and(t, GRID_W - 1)
        taps = [(dr, dc) for dr in (-1, 0, 1) for dc in (-1, 0, 1)]
    else:
        taps = [(0, dc) for dc in (-1, 0, 1)]
    for dr, dc in taps:
        delta = dr * GRID_W + dc
        w = w_ref[(dr + 1) * 3 + (dc + 1):(dr + 1) * 3 + (dc + 1) + 1, :]
        if delta == 0:
            acc = acc + w * x
            continue
        xs = pltpu.roll(x, (-delta) % T, 0)
        if two_d:
            ok = (r + dr >= 0) & (r + dr < rows) & (c + dc >= 0) & (c + dc < GRID_W)
        else:
            ok = (t + dc >= 0) & (t + dc < T)
        acc = acc + w * jnp.where(ok, xs, 0.0)
    o_ref[...] = _silu(acc)


def _conv(proj, w9, bias, prev_out, nb, T, row0, two_d, tc=256):
    rb0 = row0 // T
    return pl.pallas_call(
        functools.partial(_conv_kernel, two_d),
        grid=(nb, (2 * D_C) // tc),
        in_specs=[pl.BlockSpec((T, tc), lambda b, j: (rb0 + b, j)),
                  pl.BlockSpec((9, tc), lambda b, j: (0, j)),
                  pl.BlockSpec((1, tc), lambda b, j: (0, j)),
                  pl.BlockSpec(memory_space=pl.ANY)],
        out_specs=pl.BlockSpec((T, tc), lambda b, j: (rb0 + b, j)),
        out_shape=jax.ShapeDtypeStruct(prev_out.shape, F32),
        input_output_aliases={3: 0},
        name="mlstm_conv_T%d" % T,
    )(proj, w9, bias, prev_out)


def _mlstm_kernel(nc, has_state, want_final, *refs):
    (q_ref, k_ref, v_ref, ic_ref, fc_ref, ir_ref, fr_ref, bi_ref, bf_ref,
     tri_ref, trit_ref, causal_ref) = refs[:12]
    pos = 12
    if has_state:
        c0_ref, n0_ref, m0_ref = refs[pos:pos + 3]
        pos += 3
    pos += 1
    h_ref = refs[pos]
    pos += 1
    if want_final:
        cf_ref, nf_ref, mf_ref = refs[pos:pos + 3]
        pos += 3
    c_scr, n_scr, m_scr = refs[pos:pos + 3]
    c = pl.program_id(3)

    @pl.when(c == 0)
    def _():
        if has_state:
            c_scr[...] = c0_ref[0, 0, 0]
            n_scr[...] = n0_ref[0, 0, 0]
            m_scr[...] = m0_ref[0, 0, 0]
        else:
            c_scr[...] = jnp.zeros_like(c_scr)
            n_scr[...] = jnp.zeros_like(n_scr)
            m_scr[...] = jnp.zeros_like(m_scr)

    bi, bf = bi_ref[0, 0], bf_ref[0, 0]
    k = k_ref[...] * np.float32(DH_C ** -0.5)
    h, C, n, m = _mlstm_chunk(q_ref[...], k, v_ref[...],
                              ic_ref[0, 0] + bi, fc_ref[0, 0] + bf, ir_ref[0, 0] + bi, fr_ref[0, 0] + bf,
                              c_scr[...], n_scr[...], m_scr[...],
                              tri_ref[0], trit_ref[0], causal_ref[0])
    h_ref[0] = h
    c_scr[...] = C
    n_scr[...] = n
    m_scr[...] = m
    if want_final:
        @pl.when(c == nc - 1)
        def _():
            cf_ref[0, 0, 0] = C
            nf_ref[0, 0, 0] = n
            mf_ref[0, 0, 0] = m


def _mlstm(qk, proj, gates, consts, prev_h, nb, T, row0, state=None, want_final=False):
    ic, fc, ir, fr, bi, bf = gates
    tri, trit, causal = consts
    L = L_MLSTM
    nc = T // L
    rb0 = row0 // L
    blk = lambda b, d, c: rb0 + b * nc + c + d * (nc - 1 - 2 * c)
    tok = lambda off: pl.BlockSpec((L, DH_C), lambda b, h, d, c, off=off: (blk(b, d, c), off + h))
    gcol = pl.BlockSpec((1, 1, L, 1), lambda b, h, d, c: (d, h, blk(b, d, c), 0))
    grow = pl.BlockSpec((1, 1, 1, L), lambda b, h, d, c: (d, h, 0, blk(b, d, c)))
    gb = pl.BlockSpec((1, 1, 1, 1), lambda b, h, d, c: (d, h, 0, 0))
    cst = pl.BlockSpec((1, L, L), lambda b, h, d, c: (d, 0, 0))
    cs = pl.BlockSpec((1, 1, 1, DH_C, DH_C), lambda b, h, d, c: (b, d, h, 0, 0))
    ns = pl.BlockSpec((1, 1, 1, 1, DH_C), lambda b, h, d, c: (b, d, h, 0, 0))
    ms = pl.BlockSpec((1, 1, 1, 1, 1), lambda b, h, d, c: (b, d, h, 0, 0))
    in_specs = [tok(0), tok(H_C), tok(2 * H_C), gcol, gcol, grow, grow, gb, gb, cst, cst, cst]
    args = [qk, qk, proj, ic, fc, ir, fr, bi, bf, tri, trit, causal]
    if state is not None:
        in_specs += [cs, ns, ms]
        args += list(state)
    in_specs.append(pl.BlockSpec(memory_space=pl.ANY))
    args.append(prev_h)
    out_specs = [pl.BlockSpec((1, L, DH_C), lambda b, h, d, c: (d, blk(b, d, c), h))]
    out_shape = [jax.ShapeDtypeStruct(prev_h.shape, F32)]
    if want_final:
        out_specs += [cs, ns, ms]
        out_shape += [jax.ShapeDtypeStruct((nb, 2, H_C, DH_C, DH_C), F32),
                      jax.ShapeDtypeStruct((nb, 2, H_C, 1, DH_C), F32),
                      jax.ShapeDtypeStruct((nb, 2, H_C, 1, 1), F32)]
    return pl.pallas_call(
        functools.partial(_mlstm_kernel, nc, state is not None, want_final),
        grid=(nb, H_C, 2, nc),
        in_specs=in_specs,
        out_specs=out_specs,
        out_shape=out_shape,
        scratch_shapes=[pltpu.VMEM((DH_C, DH_C), F32), pltpu.VMEM((1, DH_C), F32), pltpu.VMEM((1, 1), F32)],
        input_output_aliases={len(args) - 1: 0},
        name="mlstm_T%d" % T,
    )(*args)


def _mlstm_combine_kernel(h_ref, o_ref, z_ref, g_ref, y_ref):
    y = _sigmoid(o_ref[...]) * (h_ref[0] + h_ref[1])
    y = y * lax.rsqrt(jnp.mean(y * y, axis=-1, keepdims=True) + EPS) * g_ref[...]
    y_ref[...] = (y * _silu(z_ref[...])).astype(BF16)


def _mlstm_combine(h2, proj, norm_g, tm=512):
    n = proj.shape[0]
    return pl.pallas_call(
        _mlstm_combine_kernel,
        grid=(n // tm, H_C),
        in_specs=[pl.BlockSpec((2, tm, DH_C), lambda i, h: (0, i, h)),
                  pl.BlockSpec((tm, DH_C), lambda i, h: (i, 3 * H_C + h)),
                  pl.BlockSpec((tm, DH_C), lambda i, h: (i, 4 * H_C + h)),
                  pl.BlockSpec((1, DH_C), lambda i, h: (0, h))],
        out_specs=pl.BlockSpec((tm, DH_C), lambda i, h: (i, h)),
        out_shape=jax.ShapeDtypeStruct((n, D_C), BF16),
        name="mlstm_combine",
    )(h2, proj, proj, norm_g)


def kernel(x_prompt, x_sample, c, state_hgrn, state_rwkv, state_mlstm_C, state_mlstm_n, state_mlstm_m, c_ctx, w_mod, b_mod, norm_g, final_norm_g, w_in_even, w_out_even, hgrn_lb_logits, hgrn_norm_g, rwkv_shift_mu, rwkv_w0, rwkv_w2, rwkv_a0, rwkv_a2, rwkv_k_k, rwkv_k_a, rwkv_r_k, rwkv_gn_g, rwkv_gn_b, w_in_odd, w_out_odd, mlstm_conv_w, mlstm_conv_b, mlstm_gate_b, mlstm_norm_g):
    dt = x_prompt.dtype
    x = jnp.concatenate([x_prompt.reshape(N_P, D_MODEL), x_sample.reshape(N_S, D_MODEL)], axis=0)

    cond8 = jnp.zeros((8, D_MODEL), F32).at[0].set(c_ctx).at[1:1 + DEC_BATCH].set(c)
    mods = _adaln(cond8, w_mod, b_mod)
    blk_cond = np.concatenate([np.zeros(N_P // ROW_BLK, np.int32),
                               1 + np.repeat(np.arange(DEC_BATCH, dtype=np.int32), DEC_SEQ // ROW_BLK)])
    mod_rows = [mods[l][blk_cond][:, None, :] for l in range(2)]

    h0 = _norm_mod(x, norm_g[0:1], mod_rows[0])
    w_in0 = jnp.pad(w_in_even[0].astype(BF16), ((0, 0), (0, IN_EVEN_PAD - IN_EVEN)))
    proj = _matmul(h0, w_in0)

    lb_all = jnp.cumsum(jax.nn.softmax(hgrn_lb_logits.astype(F32), axis=0), axis=0)
    lb = lb_all[0].reshape(H_A, 1, DK_A)
    gn_a = hgrn_norm_g[0].reshape(H_A, 1, DK_A)
    hconsts = _hgrn_consts(L_HGRN)
    s_hgrn_t = jnp.swapaxes(state_hgrn[:, 0], -1, -2)
    out_a = jnp.zeros((N_TOK, D_A), BF16)
    out_a, sfin_a = _hgrn(proj, lb, gn_a, hconsts, out_a, BATCH, SEQ, 0, want_final=True)
    (out_a,) = _hgrn(proj, lb, gn_a, hconsts, out_a, DEC_BATCH, DEC_SEQ, N_P, s0_t=s_hgrn_t)
    new_hgrn = jnp.swapaxes(sfin_a, -1, -2)[:, None].astype(dt)

    mu = rwkv_shift_mu[0]
    sh = jnp.zeros((N_TOK, SHIFT_W_B), F32)
    sh = _shift(proj, mu, sh, BATCH, SEQ, 0)
    sh = _shift(proj, mu, sh, DEC_BATCH, DEC_SEQ, N_P)
    low_pad = lambda w, off: jnp.zeros((2, 4 * R_LOW, D_B), F32).at[0, off:off + R_LOW].set(w[0]).at[
        1, off + R_LOW:off + 2 * R_LOW].set(w[1]).astype(BF16)
    bd = _head_sum_const()
    kk, lw, kd, bv, bonus = _rwkv_prep(
        sh, rwkv_w0[0].reshape(2, 1, D_B), low_pad(rwkv_w2[0], 0), rwkv_a0[0].reshape(2, 1, D_B),
        low_pad(rwkv_a2[0], 2 * R_LOW), rwkv_k_k[0].reshape(1, D_B), rwkv_k_a[0].reshape(1, D_B),
        rwkv_r_k[0].reshape(1, D_B), bd)
    tri_r = _tri_consts(L_RWKV)
    eye = np.eye(L_RWKV, dtype=np.float32)
    rconsts = (jnp.asarray(tri_r, BF16), jnp.asarray(tri_r - eye), jnp.asarray(tri_r))
    y2 = jnp.zeros((2, N_TOK, D_B), F32)
    y2, sfin_b = _rwkv(sh, kk, lw, kd, bv, rconsts, y2, BATCH, SEQ, 0, want_final=True)
    (y2,) = _rwkv(sh, kk, lw, kd, bv, rconsts, y2, DEC_BATCH, DEC_SEQ, N_P, s0=state_rwkv[:, 0])
    new_rwkv = sfin_b[:, None].astype(dt)
    out_b = _rwkv_combine(y2, bonus, proj, rwkv_gn_g[0].reshape(1, D_B), rwkv_gn_b[0].reshape(1, D_B), bd)

    x1, h1 = _out_proj(out_a, out_b, 0, 0, w_out_even[0].astype(BF16), x, mod_rows[0], norm_g[1:2],
                       next_mod_rows=mod_rows[1])

    w_in1 = w_in_odd[0].astype(BF16)
    proj1 = _matmul(h1, w_in1[:, :IN_ODD_MAIN])
    w_gate = jnp.pad(w_in1[:, IN_ODD_MAIN:], ((0, 0), (0, LANE - 4 * H_C)))
    gates = _matmul(h1, w_gate)[:, :4 * H_C]
    g4 = gates.reshape(N_TOK, 4, H_C).transpose(1, 2, 0)
    gate_args = (g4[0:2, :, :, None], g4[2:4, :, :, None], g4[0:2, :, None, :], g4[2:4, :, None, :],
                 mlstm_gate_b[0][0:2, :, None, None], mlstm_gate_b[0][2:4, :, None, None])

    w9 = mlstm_conv_w[0].reshape(9, 2 * D_C)
    cb = mlstm_conv_b[0].reshape(1, 2 * D_C)
    qk = jnp.zeros((N_TOK, 2 * D_C), F32)
    qk = _conv(proj1, w9, cb, qk, BATCH, SEQ, 0, two_d=False)
    qk = _conv(proj1, w9, cb, qk, DEC_BATCH, DEC_SEQ, N_P, two_d=True)

    tri_m = _tri_consts(L_MLSTM)
    mconsts = (jnp.asarray(tri_m, BF16), jnp.asarray(np.swapaxes(tri_m, 1, 2), BF16), jnp.asarray(tri_m))
    h2 = jnp.zeros((2, N_TOK, D_C), F32)
    h2, cfin, nfin, mfin = _mlstm(qk, proj1, gate_args, mconsts, h2, BATCH, SEQ, 0, want_final=True)
    state = (state_mlstm_C[:, 0], state_mlstm_n[:, 0][:, :, :, None, :], state_mlstm_m[:, 0][:, :, :, None, None])
    (h2,) = _mlstm(qk, proj1, gate_args, mconsts, h2, DEC_BATCH, DEC_SEQ, N_P, state=state)
    y1 = _mlstm_combine(h2, proj1, mlstm_norm_g[0].reshape(1, D_C))

    y = _out_proj(y1, y1, 0, 1, w_out_odd[0].astype(BF16), x1, mod_rows[1], final_norm_g.reshape(1, D_MODEL))

    y_prompt = y[:N_P].reshape(BATCH, SEQ, D_MODEL).astype(dt)
    y_sample = y[N_P:].reshape(DEC_BATCH, DEC_SEQ, D_MODEL).astype(dt)
    new_mlstm_C = cfin[:, None].astype(dt)
    new_mlstm_n = nfin[:, :, :, 0][:, None].astype(dt)
    new_mlstm_m = mfin[:, :, :, 0, 0][:, None].astype(dt)
    return (y_prompt, y_sample, new_hgrn, new_rwkv, new_mlstm_C, new_mlstm_n, new_mlstm_m)
```

```python
import functools

import numpy as np
import jax
import jax.numpy as jnp
from jax import lax
from jax.experimental import pallas as pl
from jax.experimental.pallas import tpu as pltpu

F32 = jnp.float32
BF16 = jnp.bfloat16

D_MODEL = 1024
BATCH, SEQ = 16, 256
DEC_BATCH, DEC_SEQ = 2, 2048
GRID_W = 64
N_P = BATCH * SEQ
N_S = DEC_BATCH * DEC_SEQ
N_TOK = N_P + N_S
D_A, DK_A, H_A = 1024, 128, 8
D_B, HS_B, H_B = 1024, 64, 16
R_LOW = 64
D_C, H_C, DH_C = 2048, 4, 512
SHIFT_W_B = 3 * D_B + 4 * R_LOW
IN_EVEN = 5 * D_A + SHIFT_W_B + D_B
IN_EVEN_PAD = 9728
IN_ODD_MAIN = 5 * D_C
EPS = 1e-6
GN_EPS_B = 64e-5

ROW_BLK = 256
L_HGRN = 64
L_RWKV = 32
SB_RWKV = 128
RWKV_VMEM_BYTES = 48 * 1024 * 1024
L_MLSTM = 256
LANE = 128

NN = (((1,), (0,)), ((), ()))
NT = (((1,), (1,)), ((), ()))
TN = (((0,), (0,)), ((), ()))


def _dot(a, b, dims=NN):
    return lax.dot_general(a.astype(BF16), b.astype(BF16), dims, preferred_element_type=F32)


def _split2(x):
    hi = x.astype(BF16)
    lo = (x - hi.astype(F32)).astype(BF16)
    return hi, lo


def _dot_sel(g, x):
    hi, lo = _split2(x)
    return (lax.dot_general(g, hi, NN, preferred_element_type=F32)
            + lax.dot_general(g, lo, NN, preferred_element_type=F32))


def _dot_sel_r(x, g):
    hi, lo = _split2(x)
    return (lax.dot_general(hi, g, NN, preferred_element_type=F32)
            + lax.dot_general(lo, g, NN, preferred_element_type=F32))


def _sigmoid(x):
    return 1.0 / (1.0 + jnp.exp(-x))


def _silu(x):
    return x * _sigmoid(x)


def _logsig(x):
    return jnp.minimum(x, 0.0) - jnp.log(1.0 + jnp.exp(-jnp.abs(x)))


def _hgrn_consts(L):
    nlev = int(np.log2(L))
    t = np.arange(L)
    G = np.zeros((nlev + 2, L, L), np.float32)
    G[0] = t[None, :] <= t[:, None]
    G[1] = t[None, :] > t[:, None]
    M = np.zeros((nlev + 1, L, L), np.float32)
    M[0] = np.eye(L)
    for lev in range(nlev):
        m = 1 << lev
        grp, pos = t // (2 * m), t % (2 * m)
        anchor = grp * 2 * m + m - 1
        for r in range(L):
            if pos[r] >= m:
                G[2 + lev, r, anchor[r] + 1:r + 1] = 1.0
            else:
                G[2 + lev, r, r + 1:anchor[r] + 1] = 1.0
        M[1 + lev] = (grp[:, None] == grp[None, :]) & (pos[:, None] >= m) & (pos[None, :] < m)
    G2 = np.stack([G.reshape(-1, L), G[:, ::-1, ::-1].reshape(-1, L)])
    M2 = np.stack([M, M[:, ::-1, ::-1]])
    return jnp.asarray(G2, BF16), jnp.asarray(M2, F32)


def _tri_consts(L):
    t = np.arange(L)
    return np.stack([t[None, :] <= t[:, None], t[None, :] >= t[:, None]]).astype(np.float32)


def _head_sum_const():
    i = np.arange(LANE)
    return jnp.asarray((i[:, None] // HS_B) == (i[None, :] // HS_B), BF16)


def _hgrn_chunks(probs, M_ref):
    L = probs[0][0].shape[0]
    nlev = M_ref.shape[1] - 1
    st = []
    for (q, v, pre, lb, St, G, d) in probs:
        f = lb + (1.0 - lb) * _sigmoid(pre)
        logf = jnp.log(f)
        st.append(dict(q=q, v=v, St=St, G=G, d=d, logf=logf, k=1.0 - f))
    for p in st:
        p["E"] = _dot_sel(p["G"], p["logf"])
    for p in st:
        p["o"] = _dot(p["q"] * jnp.exp(p["E"][0:L]), p["St"], NT)
    for p in st:
        p["att"] = M_ref[p["d"], 0] * _dot(p["q"], p["k"], NT)
    for lev in range(nlev):
        for p in st:
            F = jnp.exp(p["E"][(2 + lev) * L:(3 + lev) * L])
            p["att"] = p["att"] + M_ref[p["d"], 1 + lev] * _dot(p["q"] * F, p["k"] * F, NT)
    for p in st:
        p["o"] = p["o"] + _dot(p["att"], p["v"])
    out = []
    for p in st:
        tot = jnp.sum(p["logf"], axis=0, keepdims=True)
        St = jnp.exp(tot) * p["St"] + _dot(p["v"], p["k"] * jnp.exp(p["E"][L:2 * L]), TN)
        out.append((p["o"], St))
    return out


def _rwkv_consts(L, SB):
    t = np.arange(SB)
    same = (t[:, None] // L) == (t[None, :] // L)
    le = t[None, :] <= t[:, None]
    ge = t[None, :] >= t[:, None]
    lt = t[None, :] < t[:, None]
    gt = t[None, :] > t[:, None]
    sel = np.stack([np.concatenate([same & le, same & gt]), np.concatenate([same & ge, same & lt])])
    strict = np.stack([same & lt, same & gt])
    incl = np.stack([same & le, same & ge])
    return jnp.asarray(sel, BF16), jnp.asarray(strict, F32), jnp.asarray(incl, F32)


def _swap_heads(x):
    return pltpu.roll(x, HS_B, 1)


def _rwkv_block(L, r, lw, kd, v, vsw, kk, bv, sel, strict, incl, lm0, lm1):
    return _rwkv_blocks(L, [(r, lw, kd, v, vsw, kk, bv, sel, strict, incl)], lm0, lm1)[0]


def _rwkv_blocks(L, probs, lm0, lm1):
    SB = probs[0][0].shape[0]
    st = []
    for (r, lw, kd, v, vsw, kk, bv, sel, strict, incl) in probs:
        gg = _dot_sel(sel, lw)
        st.append(dict(r=r, lw=lw, kd=kd, v=v, vsw=vsw, kk=kk, bv=bv, strict=strict, incl=incl,
                       ginc=gg[:SB], gend=gg[SB:]))
    for p in st:
        e_inc = jnp.exp(p["ginc"])
        e_exc = jnp.exp(p["ginc"] - p["lw"])
        e_neg = jnp.exp(-p["ginc"])
        e_end = jnp.exp(p["gend"])
        p["Kq"], p["Rq"] = p["kk"] * e_exc, p["r"] * e_inc
        Bi, Ki = p["bv"] * e_neg, p["kd"] * e_neg
        p["Be"], p["Ke"] = p["bv"] * e_end, p["kd"] * e_end
        p["lhs"] = jnp.concatenate([p["Kq"] * lm0, p["Kq"] * lm1, p["Rq"] * lm0, p["Rq"] * lm1], axis=0)
        p["rhs"] = jnp.concatenate([Bi, Ki], axis=0)
    for p in st:
        p["sc"] = _dot(p["lhs"], p["rhs"], NT)
    ch = []
    for p in st:
        for h in (0, 1):
            sc = p["sc"]
            ch.append(dict(p=p, h=h,
                           Mb=p["strict"] * sc[h * SB:(h + 1) * SB, :SB],
                           Mk=p["strict"] * sc[h * SB:(h + 1) * SB, SB:],
                           Nb=p["incl"] * sc[(2 + h) * SB:(3 + h) * SB, :SB],
                           Nk=p["incl"] * sc[(2 + h) * SB:(3 + h) * SB, SB:]))
    for c in ch:
        c["MkV"] = _dot(c["Mk"], c["p"]["vsw"])
    for c in ch:
        lmh, lmo = (lm0, lm1) if c["h"] == 0 else (lm1, lm0)
        c["X"] = lmh * c["p"]["Kq"] + lmo * c["MkV"]
        c["Mp"] = c["Mb"]
    for c in ch:
        c["X"] = c["X"] - _dot(c["Mb"], c["X"])
    n = 2
    while n < L:
        for c in ch:
            c["Mp"] = _dot(c["Mp"], c["Mp"])
        for c in ch:
            c["X"] = c["X"] + _dot(c["Mp"], c["X"])
        n *= 2
    for c in ch:
        c["G"] = _dot(c["Nb"], c["X"])
    for c in ch:
        c["NkV"] = _dot(c["Nk"], c["p"]["v"])
    out = []
    for i, p in enumerate(st):
        c0, c1 = ch[2 * i], ch[2 * i + 1]
        A1 = lm0 * c0["X"] + lm1 * c1["X"]
        A2 = _swap_heads(lm0 * c1["X"] + lm1 * c0["X"])
        Qe = p["Rq"] - (lm0 * c0["G"] + lm1 * c1["G"])
        Yl = lm0 * c0["NkV"] + lm1 * c1["NkV"] - _swap_heads(lm0 * c1["G"] + lm1 * c0["G"])
        out.append((Qe, Yl, A1, A2, p["Be"], p["Ke"], p["ginc"] + p["gend"]))
    return out


def _mlstm_chunk(q, k, v, gi_col, gf_col, gi_row, gf_row, C, n, m, tri, tri_t, causal):
    L = q.shape[0]
    lf_col = _logsig(gf_col)
    lf_row = _logsig(gf_row)
    b_col = _dot_sel(tri, jnp.broadcast_to(lf_col, (L, LANE)))[:, 0:1]
    b_row = _dot_sel_r(jnp.broadcast_to(lf_row, (8, L)), tri_t)[0:1, :]
    btot = jnp.sum(lf_col, axis=0, keepdims=True)
    log_d = jnp.where(causal > 0, b_col - b_row + gi_row, -jnp.inf)
    m_loc = jnp.max(log_d, axis=1, keepdims=True)
    log_prev = b_col + m
    m_t = jnp.maximum(log_prev, m_loc)
    w_prev = jnp.exp(log_prev - m_t)
    s = _dot(q, k, NT) * jnp.exp(log_d - m_t)
    num = w_prev * _dot(q, C) + _dot(s, v)
    den = w_prev * jnp.sum(q * n, axis=1, keepdims=True) + jnp.sum(s, axis=1, keepdims=True)
    h = num / jnp.maximum(jnp.abs(den), jnp.exp(-m_t))
    log_s = btot - b_col + gi_col
    m_new = jnp.maximum(btot + m, jnp.max(log_s, axis=0, keepdims=True))
    kw = k * jnp.exp(log_s - m_new)
    w_old = jnp.exp(btot + m - m_new)
    C = w_old * C + _dot(kw, v, TN)
    n = w_old * n + jnp.sum(kw, axis=0, keepdims=True)
    return h, C, n, m_new


def _adaln_kernel(c_ref, w_ref, b_ref, o_ref):
    o_ref[0] = _dot(_silu(c_ref[...]), w_ref[0]) + b_ref[0]


def _adaln(cond8, w_mod, b_mod):
    depth, _, n3 = w_mod.shape
    tn = 1024
    return pl.pallas_call(
        _adaln_kernel,
        grid=(depth, n3 // tn),
        in_specs=[pl.BlockSpec((8, D_MODEL), lambda l, j: (0, 0)),
                  pl.BlockSpec((1, D_MODEL, tn), lambda l, j: (l, 0, j)),
                  pl.BlockSpec((1, 1, tn), lambda l, j: (l, 0, j))],
        out_specs=pl.BlockSpec((1, 8, tn), lambda l, j: (l, 0, j)),
        out_shape=jax.ShapeDtypeStruct((depth, 8, n3), F32),
        name="adaln",
    )(cond8, w_mod, b_mod.reshape(depth, 1, n3))


def _rms_mod(x, g, mod):
    y = x * lax.rsqrt(jnp.mean(x * x, axis=-1, keepdims=True) + EPS) * g
    return y * (1.0 + mod[:, D_MODEL:2 * D_MODEL]) + mod[:, 0:D_MODEL]


def _norm_mod_kernel(x_ref, g_ref, mod_ref, h_ref):
    h_ref[...] = _rms_mod(x_ref[...], g_ref[...], mod_ref[0]).astype(BF16)


def _norm_mod(x, g, mod_rows):
    nblk = x.shape[0] // ROW_BLK
    return pl.pallas_call(
        _norm_mod_kernel,
        grid=(nblk,),
        in_specs=[pl.BlockSpec((ROW_BLK, D_MODEL), lambda i: (i, 0)),
                  pl.BlockSpec((1, D_MODEL), lambda i: (0, 0)),
                  pl.BlockSpec((1, 1, 3 * D_MODEL), lambda i: (i, 0, 0))],
        out_specs=pl.BlockSpec((ROW_BLK, D_MODEL), lambda i: (i, 0)),
        out_shape=jax.ShapeDtypeStruct(x.shape, BF16),
        name="norm_mod",
    )(x, g, mod_rows)


def _matmul_kernel(a_ref, w_ref, o_ref):
    o_ref[...] = jnp.dot(a_ref[...], w_ref[...], preferred_element_type=F32)


def _matmul(a, w, tm=1024, tn=512):
    m, k = a.shape
    n = w.shape[1]
    tn = min(tn, n)
    return pl.pallas_call(
        _matmul_kernel,
        grid=(m // tm, n // tn),
        in_specs=[pl.BlockSpec((tm, k), lambda i, j: (i, 0)),
                  pl.BlockSpec((k, tn), lambda i, j: (0, j))],
        out_specs=pl.BlockSpec((tm, tn), lambda i, j: (i, j)),
        out_shape=jax.ShapeDtypeStruct((m, n), F32),
        name="in_proj",
    )(a, w)


def _hgrn_kernel(T, has_state, want_final, *refs):
    q_ref, v_ref, ff_ref, fb_ref, z_ref, lb_ref, gn_ref, G_ref, M_ref = refs[:9]
    pos = 9
    s0_ref = None
    if has_state:
        s0_ref = refs[pos]
        pos += 1
    pos += 1
    out_ref = refs[pos]
    pos += 1
    sfin_ref = None
    if want_final:
        sfin_ref = refs[pos]
        pos += 1
    o_dir = refs[pos]
    L = L_HGRN
    nc = T // L
    lb = lb_ref[0]
    f_refs = (ff_ref, fb_ref)

    def step(c, carry):
        probs = []
        for d in (0, 1):
            ci = c if d == 0 else nc - 1 - c
            rows = pl.ds(pl.multiple_of(ci * L, L), L)
            probs.append((q_ref[rows, :], v_ref[rows, :], f_refs[d][rows, :], lb, carry[d], G_ref[d], d))
        res = _hgrn_chunks(probs, M_ref)
        for d in (0, 1):
            ci = c if d == 0 else nc - 1 - c
            o_dir[d, pl.ds(pl.multiple_of(ci * L, L), L), :] = res[d][0]
        return (res[0][1], res[1][1])

    if has_state:
        init = (s0_ref[0, 0, 0], s0_ref[0, 1, 0])
    else:
        init = (jnp.zeros((DK_A, DK_A), F32), jnp.zeros((DK_A, DK_A), F32))
    fin = lax.fori_loop(0, nc, step, init)
    if want_final:
        sfin_ref[0, 0, 0] = fin[0]
        sfin_ref[0, 1, 0] = fin[1]
    o = o_dir[0] + o_dir[1]
    y = o * lax.rsqrt(jnp.mean(o * o, axis=-1, keepdims=True) + EPS) * gn_ref[0]
    out_ref[...] = (y * _silu(z_ref[...])).astype(BF16)


def _hgrn(proj, lb, gn, consts, prev_out, nb, T, row0, s0_t=None, want_final=False):
    G, M = consts
    rb0 = row0 // T
    col = lambda off: pl.BlockSpec((T, LANE), lambda b, h, off=off: (rb0 + b, off + h))
    in_specs = [col(0), col(8), col(16), col(24), col(32),
                pl.BlockSpec((1, 1, LANE), lambda b, h: (h, 0, 0)),
                pl.BlockSpec((1, 1, LANE), lambda b, h: (h, 0, 0)),
                pl.BlockSpec(G.shape, lambda b, h: (0, 0, 0)),
                pl.BlockSpec(M.shape, lambda b, h: (0, 0, 0, 0))]
    args = [proj, proj, proj, proj, proj, lb, gn, G, M]
    if s0_t is not None:
        in_specs.append(pl.BlockSpec((1, 2, 1, DK_A, DK_A), lambda b, h: (b, 0, h, 0, 0)))
        args.append(s0_t)
    in_specs.append(pl.BlockSpec(memory_space=pl.ANY))
    args.append(prev_out)
    out_specs = [pl.BlockSpec((T, LANE), lambda b, h: (rb0 + b, h))]
    out_shape = [jax.ShapeDtypeStruct(prev_out.shape, BF16)]
    if want_final:
        out_specs.append(pl.BlockSpec((1, 2, 1, DK_A, DK_A), lambda b, h: (b, 0, h, 0, 0)))
        out_shape.append(jax.ShapeDtypeStruct((nb, 2, H_A, DK_A, DK_A), F32))
    res = pl.pallas_call(
        functools.partial(_hgrn_kernel, T, s0_t is not None, want_final),
        grid=(nb, H_A),
        in_specs=in_specs,
        out_specs=out_specs,
        out_shape=out_shape,
        scratch_shapes=[pltpu.VMEM((2, T, LANE), F32)],
        input_output_aliases={len(args) - 1: 0},
        name="hgrn2_T%d" % T,
    )(*args)
    return res


def _shift_kernel(p_ref, mu_ref, prev_ref, o_ref):
    p = p_ref[...]
    T = p.shape[0]
    t = lax.broadcasted_iota(jnp.int32, p.shape, 0)
    prev = jnp.where(t == 0, 0.0, pltpu.roll(p, 1, 0))
    nxt = jnp.where(t == T - 1, 0.0, pltpu.roll(p, T - 1, 0))
    mu = mu_ref[...]
    o_ref[...] = p + mu[0:1] * (prev - p) + mu[1:2] * (nxt - p)


def _shift(proj, mu, prev_out, nb, T, row0):
    tc = 256
    rb0 = row0 // T
    c0 = (5 * D_A) // tc
    return pl.pallas_call(
        _shift_kernel,
        grid=(nb, SHIFT_W_B // tc),
        in_specs=[pl.BlockSpec((T, tc), lambda b, j: (rb0 + b, c0 + j)),
                  pl.BlockSpec((2, tc), lambda b, j: (0, j)),
                  pl.BlockSpec(memory_space=pl.ANY)],
        out_specs=pl.BlockSpec((T, tc), lambda b, j: (rb0 + b, j)),
        out_shape=jax.ShapeDtypeStruct(prev_out.shape, F32),
        input_output_aliases={2: 0},
        name="rwkv_shift_T%d" % T,
    )(proj, mu, prev_out)


def _rwkv_prep_kernel(r_ref, k_ref, v_ref, low_ref, w0_ref, w2_ref, a0_ref, a2_ref, kk_ref_p, ka_ref, rk_ref,
                      bd_ref, kk_ref, lw_ref, kd_ref, bv_ref, bonus_ref):
    r, k, v, low = r_ref[...], k_ref[...], v_ref[...], low_ref[...]
    bd = bd_ref[...]
    tl = jnp.tanh(low)
    kkr = k * kk_ref_p[...]
    kk = kkr / jnp.maximum(jnp.sqrt(_dot_sel_r(kkr * kkr, bd)), 1e-12)
    kk_ref[...] = kk
    bonus = jnp.zeros_like(r)
    for d in (0, 1):
        u = w0_ref[d] + _dot(tl, w2_ref[d])
        lw_ref[d] = -np.float32(np.exp(-0.5)) * _sigmoid(u)
        a = _sigmoid(a0_ref[d] + _dot(low, a2_ref[d]))
        kd = k * (1.0 + (a - 1.0) * ka_ref[...])
        kd_ref[d] = kd
        bv_ref[d] = a * kk
        bonus = bonus + _dot_sel_r(r * kd * rk_ref[...], bd) * v
    bonus_ref[...] = bonus


def _rwkv_prep(sh, w0, w2p, a0, a2p, k_k, k_a, r_k, bd, tm=1024):
    n = sh.shape[0]
    tile = lambda off: pl.BlockSpec((tm, LANE), lambda i, j, off=off: (i, off + j))
    vec = pl.BlockSpec((1, LANE), lambda i, j: (0, j))
    dvec = pl.BlockSpec((2, 1, LANE), lambda i, j: (0, 0, j))
    dmat = pl.BlockSpec((2, 4 * R_LOW, LANE), lambda i, j: (0, 0, j))
    dout = pl.BlockSpec((2, tm, LANE), lambda i, j: (0, i, j))
    one = jax.ShapeDtypeStruct((n, D_B), F32)
    two = jax.ShapeDtypeStruct((2, n, D_B), F32)
    return pl.pallas_call(
        _rwkv_prep_kernel,
        grid=(n // tm, D_B // LANE),
        in_specs=[tile(0), tile(8), tile(16),
                  pl.BlockSpec((tm, 4 * R_LOW), lambda i, j: (i, (3 * D_B) // (4 * R_LOW))),
                  dvec, dmat, dvec, dmat, vec, vec, vec,
                  pl.BlockSpec((LANE, LANE), lambda i, j: (0, 0))],
        out_specs=[tile(0), dout, dout, dout, tile(0)],
        out_shape=[one, two, two, two, one],
        name="rwkv_prep",
    )(sh, sh, sh, sh, w0, w2p, a0, a2p, k_k, k_a, r_k, bd)


def _rwkv_kernel(T, has_state, want_final, *refs):
    (r_ref, v_ref, kk_ref, lw_ref, kd_ref, bv_ref, sel_ref, strict_ref, incl_ref, bdeye_ref) = refs[:10]
    pos = 10
    s0_ref = None
    if has_state:
        s0_ref = refs[pos]
        pos += 1
    pos += 1
    y_ref = refs[pos]
    pos += 1
    sfin_ref = None
    if want_final:
        sfin_ref = refs[pos]
        pos += 1
    tm_scr, u_scr, qe_scr, s_scr = refs[pos:pos + 4]
    L, SB = L_RWKV, SB_RWKV
    nc = T // L
    cps = SB // L

    def block_step(sb, _):
        lane = lax.broadcasted_iota(jnp.int32, (1, LANE), 1)
        lm0 = (lane < HS_B).astype(F32)
        lm1 = 1.0 - lm0
        rows = pl.ds(pl.multiple_of(sb * SB, SB), SB)
        r, v, kk = r_ref[rows, :], v_ref[rows, :], kk_ref[rows, :]
        vsw = _swap_heads(v)
        res = _rwkv_blocks(L, [(r, lw_ref[d, rows, :], kd_ref[d, rows, :], v, vsw, kk, bv_ref[d, rows, :],
                                sel_ref[d], strict_ref[d], incl_ref[d]) for d in (0, 1)], lm0, lm1)
        for d in (0, 1):
            y_ref[d, rows, :] = res[d][1]
            qe_scr[d, rows, :] = res[d][0].astype(BF16)
        for j in range(cps):
            rc = slice(j * L, (j + 1) * L)
            for d in (0, 1):
                _, _, A1, A2, Be, Ke, gt = res[d]
                decay = bdeye_ref[1] * jnp.exp(gt[j * L:j * L + 1])
                tm_scr[d, sb * cps + j] = (decay - bdeye_ref[0] * _dot(A1[rc], Be[rc], TN)).astype(BF16)
            for d in (0, 1):
                _, _, A1, A2, Be, Ke, gt = res[d]
                u_scr[d, sb * cps + j] = bdeye_ref[0] * _dot(
                    jnp.concatenate([v[rc], -A2[rc]], axis=0), jnp.concatenate([Ke[rc], Be[rc]], axis=0), TN)
        return 0

    lax.fori_loop(0, T // SB, block_step, 0)

    def chunk_step(c, _):
        for d in (0, 1):
            ci = c if d == 0 else nc - 1 - c
            rows = pl.ds(pl.multiple_of(ci * L, L), L)
            S = s_scr[d].astype(BF16)
            y_ref[d, rows, :] = y_ref[d, rows, :] + lax.dot_general(qe_scr[d, rows, :], S, NT,
                                                                    preferred_element_type=F32)
            s_scr[d] = lax.dot_general(S, tm_scr[d, ci], NN, preferred_element_type=F32) + u_scr[d, ci]
        return 0

    zero = jnp.zeros((HS_B, HS_B), F32)
    for d in (0, 1):
        if has_state:
            top = jnp.concatenate([s0_ref[0, d, 0], zero], axis=1)
            bot = jnp.concatenate([zero, s0_ref[0, d, 1]], axis=1)
            s_scr[d] = jnp.concatenate([top, bot], axis=0)
        else:
            s_scr[d] = jnp.zeros((LANE, LANE), F32)
    lax.fori_loop(0, nc, chunk_step, 0)
    if want_final:
        for d in (0, 1):
            fin = s_scr[d]
            sfin_ref[0, d, 0] = fin[0:HS_B, 0:HS_B]
            sfin_ref[0, d, 1] = fin[HS_B:LANE, HS_B:LANE]


def _rwkv(sh, kk, lw, kd, bv, consts, prev_y, nb, T, row0, s0=None, want_final=False):
    sel, strict, incl, bdeye = consts
    rb0 = row0 // T
    nhp = H_B // 2
    nc = T // L_RWKV
    tok = lambda off: pl.BlockSpec((T, LANE), lambda b, p, off=off: (rb0 + b, off + p))
    dtok = pl.BlockSpec((2, T, LANE), lambda b, p: (0, rb0 + b, p))
    cst = lambda a: pl.BlockSpec(a.shape, lambda b, p: (0, 0, 0))
    st = pl.BlockSpec((1, 2, 2, HS_B, HS_B), lambda b, p: (b, 0, p, 0, 0))
    in_specs = [tok(0), tok(2 * D_B // LANE), tok(0), dtok, dtok, dtok, cst(sel), cst(strict), cst(incl), cst(bdeye)]
    args = [sh, sh, kk, lw, kd, bv, sel, strict, incl, bdeye]
    if s0 is not None:
        in_specs.append(st)
        args.append(s0)
    in_specs.append(pl.BlockSpec(memory_space=pl.ANY))
    args.append(prev_y)
    out_specs = [dtok]
    out_shape = [jax.ShapeDtypeStruct(prev_y.shape, F32)]
    if want_final:
        out_specs.append(st)
        out_shape.append(jax.ShapeDtypeStruct((nb, 2, H_B, HS_B, HS_B), F32))
    return pl.pallas_call(
        functools.partial(_rwkv_kernel, T, s0 is not None, want_final),
        grid=(nb, nhp),
        in_specs=in_specs,
        out_specs=out_specs,
        out_shape=out_shape,
        scratch_shapes=[pltpu.VMEM((2, nc, LANE, LANE), BF16), pltpu.VMEM((2, nc, LANE, LANE), F32),
                        pltpu.VMEM((2, T, LANE), BF16), pltpu.VMEM((2, LANE, LANE), F32)],
        input_output_aliases={len(args) - 1: 0},
        compiler_params=pltpu.CompilerParams(vmem_limit_bytes=RWKV_VMEM_BYTES),
        name="rwkv7_T%d" % T,
    )(*args)


def _rwkv_combine_kernel(y_ref, bonus_ref, z_ref, g_ref, b_ref, bd_ref, o_ref):
    bd = bd_ref[...]
    y = y_ref[0] + y_ref[1]
    yc = y - _dot_sel_r(y, bd) * (1.0 / HS_B)
    var = _dot_sel_r(yc * yc, bd) * (1.0 / HS_B)
    yn = yc * lax.rsqrt(var + GN_EPS_B) * g_ref[...] + b_ref[...]
    o_ref[...] = ((yn + bonus_ref[...]) * _silu(z_ref[...])).astype(BF16)


def _rwkv_combine(y2, bonus, proj, gn_g, gn_b, bd, tm=1024):
    n = bonus.shape[0]
    zc0 = (5 * D_A + SHIFT_W_B) // LANE
    tile = pl.BlockSpec((tm, LANE), lambda i, j: (i, j))
    vec = pl.BlockSpec((1, LANE), lambda i, j: (0, j))
    return pl.pallas_call(
        _rwkv_combine_kernel,
        grid=(n // tm, D_B // LANE),
        in_specs=[pl.BlockSpec((2, tm, LANE), lambda i, j: (0, i, j)), tile,
                  pl.BlockSpec((tm, LANE), lambda i, j: (i, zc0 + j)), vec, vec,
                  pl.BlockSpec((LANE, LANE), lambda i, j: (0, 0))],
        out_specs=tile,
        out_shape=jax.ShapeDtypeStruct((n, D_B), BF16),
        name="rwkv_combine",
    )(y2, bonus, proj, gn_g, gn_b, bd)


def _out_proj_kernel(final, a_ref, b_ref, w_ref, x_ref, mod_ref, g_ref, *rest):
    acc = jnp.dot(a_ref[...], w_ref[0:D_MODEL, :], preferred_element_type=F32)
    acc = acc + jnp.dot(b_ref[...], w_ref[D_MODEL:2 * D_MODEL, :], preferred_element_type=F32)
    x = x_ref[...] + mod_ref[0][:, 2 * D_MODEL:3 * D_MODEL] * acc
    if final:
        (y_ref,) = rest
        y_ref[...] = x * lax.rsqrt(jnp.mean(x * x, axis=-1, keepdims=True) + EPS) * g_ref[...]
    else:
        nmod_ref, x_out_ref, h_ref = rest
        x_out_ref[...] = x
        h_ref[...] = _rms_mod(x, g_ref[...], nmod_ref[0]).astype(BF16)


def _out_proj(a, b, a_col, b_col, w, x, mod_rows, g, next_mod_rows=None):
    n = x.shape[0]
    nblk = n // ROW_BLK
    final = next_mod_rows is None
    row = lambda c: pl.BlockSpec((ROW_BLK, D_MODEL), lambda i, c=c: (i, c))
    modspec = pl.BlockSpec((1, 1, 3 * D_MODEL), lambda i: (i, 0, 0))
    in_specs = [row(a_col), row(b_col), pl.BlockSpec(w.shape, lambda i: (0, 0)), row(0), modspec,
                pl.BlockSpec((1, D_MODEL), lambda i: (0, 0))]
    args = [a, b, w, x, mod_rows, g]
    if final:
        out_specs = row(0)
        out_shape = jax.ShapeDtypeStruct((n, D_MODEL), F32)
    else:
        in_specs.append(modspec)
        args.append(next_mod_rows)
        out_specs = [row(0), row(0)]
        out_shape = [jax.ShapeDtypeStruct((n, D_MODEL), F32), jax.ShapeDtypeStruct((n, D_MODEL), BF16)]
    return pl.pallas_call(
        functools.partial(_out_proj_kernel, final),
        grid=(nblk,),
        in_specs=in_specs,
        out_specs=out_specs,
        out_shape=out_shape,
        name="out_proj_final" if final else "out_proj",
    )(*args)


def _conv_kernel(two_d, x_ref, w_ref, b_ref, prev_ref, o_ref):
    x = x_ref[...]
    T = x.shape[0]
    t = lax.broadcasted_iota(jnp.int32, x.shape, 0)
    acc = jnp.zeros_like(x) + b_ref[...]
    if two_d:
        rows = T // GRID_W
        r, c = jnp.right_shift(t, int(np.log2(GRID_W))), jnp.bitwise_and(t, GRID_W - 1)
        taps = [(dr, dc) for dr in (-1, 0, 1) for dc in (-1, 0, 1)]
    else:
        taps = [(0, dc) for dc in (-1, 0, 1)]
    for dr, dc in taps:
        delta = dr * GRID_W + dc
        w = w_ref[(dr + 1) * 3 + (dc + 1):(dr + 1) * 3 + (dc + 1) + 1, :]
        if delta == 0:
            acc = acc + w * x
            continue
        xs = pltpu.roll(x, (-delta) % T, 0)
        if two_d:
            ok = (r + dr >= 0) & (r + dr < rows) & (c + dc >= 0) & (c + dc < GRID_W)
        else:
            ok = (t + dc >= 0) & (t + dc < T)
        acc = acc + w * jnp.where(ok, xs, 0.0)
    o_ref[...] = _silu(acc)


def _conv(proj, w9, bias, prev_out, nb, T, row0, two_d, tc=256):
    rb0 = row0 // T
    return pl.pallas_call(
        functools.partial(_conv_kernel, two_d),
        grid=(nb, (2 * D_C) // tc),
        in_specs=[pl.BlockSpec((T, tc), lambda b, j: (rb0 + b, j)),
                  pl.BlockSpec((9, tc), lambda b, j: (0, j)),
                  pl.BlockSpec((1, tc), lambda b, j: (0, j)),
                  pl.BlockSpec(memory_space=pl.ANY)],
        out_specs=pl.BlockSpec((T, tc), lambda b, j: (rb0 + b, j)),
        out_shape=jax.ShapeDtypeStruct(prev_out.shape, F32),
        input_output_aliases={3: 0},
        name="mlstm_conv_T%d" % T,
    )(proj, w9, bias, prev_out)


def _mlstm_kernel(nc, has_state, want_final, *refs):
    (q_ref, k_ref, v_ref, ic_ref, fc_ref, ir_ref, fr_ref, bi_ref, bf_ref,
     tri_ref, trit_ref, causal_ref) = refs[:12]
    pos = 12
    if has_state:
        c0_ref, n0_ref, m0_ref = refs[pos:pos + 3]
        pos += 3
    pos += 1
    h_ref = refs[pos]
    pos += 1
    if want_final:
        cf_ref, nf_ref, mf_ref = refs[pos:pos + 3]
        pos += 3
    c_scr, n_scr, m_scr = refs[pos:pos + 3]
    c = pl.program_id(3)

    @pl.when(c == 0)
    def _():
        if has_state:
            c_scr[...] = c0_ref[0, 0, 0]
            n_scr[...] = n0_ref[0, 0, 0]
            m_scr[...] = m0_ref[0, 0, 0]
        else:
            c_scr[...] = jnp.zeros_like(c_scr)
            n_scr[...] = jnp.zeros_like(n_scr)
            m_scr[...] = jnp.zeros_like(m_scr)

    bi, bf = bi_ref[0, 0], bf_ref[0, 0]
    k = k_ref[...] * np.float32(DH_C ** -0.5)
    h, C, n, m = _mlstm_chunk(q_ref[...], k, v_ref[...],
                              ic_ref[0, 0] + bi, fc_ref[0, 0] + bf, ir_ref[0, 0] + bi, fr_ref[0, 0] + bf,
                              c_scr[...], n_scr[...], m_scr[...],
                              tri_ref[0], trit_ref[0], causal_ref[0])
    h_ref[0] = h
    c_scr[...] = C
    n_scr[...] = n
    m_scr[...] = m
    if want_final:
        @pl.when(c == nc - 1)
        def _():
            cf_ref[0, 0, 0] = C
            nf_ref[0, 0, 0] = n
            mf_ref[0, 0, 0] = m


def _mlstm(qk, proj, gates, consts, prev_h, nb, T, row0, state=None, want_final=False):
    ic, fc, ir, fr, bi, bf = gates
    tri, trit, causal = consts
    L = L_MLSTM
    nc = T // L
    rb0 = row0 // L
    blk = lambda b, d, c: rb0 + b * nc + c + d * (nc - 1 - 2 * c)
    tok = lambda off: pl.BlockSpec((L, DH_C), lambda b, h, d, c, off=off: (blk(b, d, c), off + h))
    gcol = pl.BlockSpec((1, 1, L, 1), lambda b, h, d, c: (d, h, blk(b, d, c), 0))
    grow = pl.BlockSpec((1, 1, 1, L), lambda b, h, d, c: (d, h, 0, blk(b, d, c)))
    gb = pl.BlockSpec((1, 1, 1, 1), lambda b, h, d, c: (d, h, 0, 0))
    cst = pl.BlockSpec((1, L, L), lambda b, h, d, c: (d, 0, 0))
    cs = pl.BlockSpec((1, 1, 1, DH_C, DH_C), lambda b, h, d, c: (b, d, h, 0, 0))
    ns = pl.BlockSpec((1, 1, 1, 1, DH_C), lambda b, h, d, c: (b, d, h, 0, 0))
    ms = pl.BlockSpec((1, 1, 1, 1, 1), lambda b, h, d, c: (b, d, h, 0, 0))
    in_specs = [tok(0), tok(H_C), tok(2 * H_C), gcol, gcol, grow, grow, gb, gb, cst, cst, cst]
    args = [qk, qk, proj, ic, fc, ir, fr, bi, bf, tri, trit, causal]
    if state is not None:
        in_specs += [cs, ns, ms]
        args += list(state)
    in_specs.append(pl.BlockSpec(memory_space=pl.ANY))
    args.append(prev_h)
    out_specs = [pl.BlockSpec((1, L, DH_C), lambda b, h, d, c: (d, blk(b, d, c), h))]
    out_shape = [jax.ShapeDtypeStruct(prev_h.shape, F32)]
    if want_final:
        out_specs += [cs, ns, ms]
        out_shape += [jax.ShapeDtypeStruct((nb, 2, H_C, DH_C, DH_C), F32),
                      jax.ShapeDtypeStruct((nb, 2, H_C, 1, DH_C), F32),
                      jax.ShapeDtypeStruct((nb, 2, H_C, 1, 1), F32)]
    return pl.pallas_call(
        functools.partial(_mlstm_kernel, nc, state is not None, want_final),
        grid=(nb, H_C, 2, nc),
        in_specs=in_specs,
        out_specs=out_specs,
        out_shape=out_shape,
        scratch_shapes=[pltpu.VMEM((DH_C, DH_C), F32), pltpu.VMEM((1, DH_C), F32), pltpu.VMEM((1, 1), F32)],
        input_output_aliases={len(args) - 1: 0},
        name="mlstm_T%d" % T,
    )(*args)


def _mlstm_combine_kernel(h_ref, o_ref, z_ref, g_ref, y_ref):
    y = _sigmoid(o_ref[...]) * (h_ref[0] + h_ref[1])
    y = y * lax.rsqrt(jnp.mean(y * y, axis=-1, keepdims=True) + EPS) * g_ref[...]
    y_ref[...] = (y * _silu(z_ref[...])).astype(BF16)


def _mlstm_combine(h2, proj, norm_g, tm=512):
    n = proj.shape[0]
    return pl.pallas_call(
        _mlstm_combine_kernel,
        grid=(n // tm, H_C),
        in_specs=[pl.BlockSpec((2, tm, DH_C), lambda i, h: (0, i, h)),
                  pl.BlockSpec((tm, DH_C), lambda i, h: (i, 3 * H_C + h)),
                  pl.BlockSpec((tm, DH_C), lambda i, h: (i, 4 * H_C + h)),
                  pl.BlockSpec((1, DH_C), lambda i, h: (0, h))],
        out_specs=pl.BlockSpec((tm, DH_C), lambda i, h: (i, h)),
        out_shape=jax.ShapeDtypeStruct((n, D_C), BF16),
        name="mlstm_combine",
    )(h2, proj, proj, norm_g)


def kernel(x_prompt, x_sample, c, state_hgrn, state_rwkv, state_mlstm_C, state_mlstm_n, state_mlstm_m, c_ctx, w_mod, b_mod, norm_g, final_norm_g, w_in_even, w_out_even, hgrn_lb_logits, hgrn_norm_g, rwkv_shift_mu, rwkv_w0, rwkv_w2, rwkv_a0, rwkv_a2, rwkv_k_k, rwkv_k_a, rwkv_r_k, rwkv_gn_g, rwkv_gn_b, w_in_odd, w_out_odd, mlstm_conv_w, mlstm_conv_b, mlstm_gate_b, mlstm_norm_g):
    dt = x_prompt.dtype
    x = jnp.concatenate([x_prompt.reshape(N_P, D_MODEL), x_sample.reshape(N_S, D_MODEL)], axis=0)

    cond8 = jnp.zeros((8, D_MODEL), F32).at[0].set(c_ctx).at[1:1 + DEC_BATCH].set(c)
    mods = _adaln(cond8, w_mod, b_mod)
    blk_cond = np.concatenate([np.zeros(N_P // ROW_BLK, np.int32),
                               1 + np.repeat(np.arange(DEC_BATCH, dtype=np.int32), DEC_SEQ // ROW_BLK)])
    mod_rows = [mods[l][blk_cond][:, None, :] for l in range(2)]

    h0 = _norm_mod(x, norm_g[0:1], mod_rows[0])
    w_in0 = jnp.pad(w_in_even[0].astype(BF16), ((0, 0), (0, IN_EVEN_PAD - IN_EVEN)))
    proj = _matmul(h0, w_in0)

    lb_all = jnp.cumsum(jax.nn.softmax(hgrn_lb_logits.astype(F32), axis=0), axis=0)
    lb = lb_all[0].reshape(H_A, 1, DK_A)
    gn_a = hgrn_norm_g[0].reshape(H_A, 1, DK_A)
    hconsts = _hgrn_consts(L_HGRN)
    s_hgrn_t = jnp.swapaxes(state_hgrn[:, 0], -1, -2)
    out_a = jnp.zeros((N_TOK, D_A), BF16)
    out_a, sfin_a = _hgrn(proj, lb, gn_a, hconsts, out_a, BATCH, SEQ, 0, want_final=True)
    (out_a,) = _hgrn(proj, lb, gn_a, hconsts, out_a, DEC_BATCH, DEC_SEQ, N_P, s0_t=s_hgrn_t)
    new_hgrn = jnp.swapaxes(sfin_a, -1, -2)[:, None].astype(dt)

    mu = rwkv_shift_mu[0]
    sh = jnp.zeros((N_TOK, SHIFT_W_B), F32)
    sh = _shift(proj, mu, sh, BATCH, SEQ, 0)
    sh = _shift(proj, mu, sh, DEC_BATCH, DEC_SEQ, N_P)
    low_pad = lambda w, off: jnp.zeros((2, 4 * R_LOW, D_B), F32).at[0, off:off + R_LOW].set(w[0]).at[
        1, off + R_LOW:off + 2 * R_LOW].set(w[1]).astype(BF16)
    bd = _head_sum_const()
    kk, lw, kd, bv, bonus = _rwkv_prep(
        sh, rwkv_w0[0].reshape(2, 1, D_B), low_pad(rwkv_w2[0], 0), rwkv_a0[0].reshape(2, 1, D_B),
        low_pad(rwkv_a2[0], 2 * R_LOW), rwkv_k_k[0].reshape(1, D_B), rwkv_k_a[0].reshape(1, D_B),
        rwkv_r_k[0].reshape(1, D_B), bd)
    lane_head = np.arange(LANE) // HS_B
    bdeye = jnp.asarray(np.stack([lane_head[:, None] == lane_head[None, :], np.eye(LANE, dtype=bool)]), F32)
    rconsts = _rwkv_consts(L_RWKV, SB_RWKV) + (bdeye,)
    y2 = jnp.zeros((2, N_TOK, D_B), F32)
    y2, sfin_b = _rwkv(sh, kk, lw, kd, bv, rconsts, y2, BATCH, SEQ, 0, want_final=True)
    (y2,) = _rwkv(sh, kk, lw, kd, bv, rconsts, y2, DEC_BATCH, DEC_SEQ, N_P, s0=state_rwkv[:, 0])
    new_rwkv = sfin_b[:, None].astype(dt)
    out_b = _rwkv_combine(y2, bonus, proj, rwkv_gn_g[0].reshape(1, D_B), rwkv_gn_b[0].reshape(1, D_B), bd)

    x1, h1 = _out_proj(out_a, out_b, 0, 0, w_out_even[0].astype(BF16), x, mod_rows[0], norm_g[1:2],
                       next_mod_rows=mod_rows[1])

    w_in1 = w_in_odd[0].astype(BF16)
    proj1 = _matmul(h1, w_in1[:, :IN_ODD_MAIN])
    w_gate = jnp.pad(w_in1[:, IN_ODD_MAIN:], ((0, 0), (0, LANE - 4 * H_C)))
    gates = _matmul(h1, w_gate)[:, :4 * H_C]
    g4 = gates.reshape(N_TOK, 4, H_C).transpose(1, 2, 0)
    gate_args = (g4[0:2, :, :, None], g4[2:4, :, :, None], g4[0:2, :, None, :], g4[2:4, :, None, :],
                 mlstm_gate_b[0][0:2, :, None, None], mlstm_gate_b[0][2:4, :, None, None])

    w9 = mlstm_conv_w[0].reshape(9, 2 * D_C)
    cb = mlstm_conv_b[0].reshape(1, 2 * D_C)
    qk = jnp.zeros((N_TOK, 2 * D_C), F32)
    qk = _conv(proj1, w9, cb, qk, BATCH, SEQ, 0, two_d=False)
    qk = _conv(proj1, w9, cb, qk, DEC_BATCH, DEC_SEQ, N_P, two_d=True)

    tri_m = _tri_consts(L_MLSTM)
    mconsts = (jnp.asarray(tri_m, BF16), jnp.asarray(np.swapaxes(tri_m, 1, 2), BF16), jnp.asarray(tri_m))
    h2 = jnp.zeros((2, N_TOK, D_C), F32)
    h2, cfin, nfin, mfin = _mlstm(qk, proj1, gate_args, mconsts, h2, BATCH, SEQ, 0, want_final=True)
    state = (state_mlstm_C[:, 0], state_mlstm_n[:, 0][:, :, :, None, :], state_mlstm_m[:, 0][:, :, :, None, None])
    (h2,) = _mlstm(qk, proj1, gate_args, mconsts, h2, DEC_BATCH, DEC_SEQ, N_P, state=state)
    y1 = _mlstm_combine(h2, proj1, mlstm_norm_g[0].reshape(1, D_C))

    y = _out_proj(y1, y1, 0, 1, w_out_odd[0].astype(BF16), x1, mod_rows[1], final_norm_g.reshape(1, D_MODEL))

    y_prompt = y[:N_P].reshape(BATCH, SEQ, D_MODEL).astype(dt)
    y_sample = y[N_P:].reshape(DEC_BATCH, DEC_SEQ, D_MODEL).astype(dt)
    new_mlstm_C = cfin[:, None].astype(dt)
    new_mlstm_n = nfin[:, :, :, 0][:, None].astype(dt)
    new_mlstm_m = mfin[:, :, :, 0, 0][:, None].astype(dt)
    return (y_prompt, y_sample, new_hgrn, new_rwkv, new_mlstm_C, new_mlstm_n, new_mlstm_m)
```

```python
import functools

import numpy as np
import jax
import jax.numpy as jnp
from jax import lax
from jax.experimental import pallas as pl
from jax.experimental.pallas import tpu as pltpu

F32 = jnp.float32
BF16 = jnp.bfloat16

D_MODEL = 1024
BATCH, SEQ = 16, 256
DEC_BATCH, DEC_SEQ = 2, 2048
GRID_W = 64
N_P = BATCH * SEQ
N_S = DEC_BATCH * DEC_SEQ
N_TOK = N_P + N_S
D_A, DK_A, H_A = 1024, 128, 8
D_B, HS_B, H_B = 1024, 64, 16
R_LOW = 64
D_C, H_C, DH_C = 2048, 4, 512
SHIFT_W_B = 3 * D_B + 4 * R_LOW
IN_EVEN = 5 * D_A + SHIFT_W_B + D_B
IN_EVEN_PAD = 9728
IN_ODD_MAIN = 5 * D_C
EPS = 1e-6
GN_EPS_B = 64e-5

ROW_BLK = 256
SEQ_BLK = 2048
L_HGRN = 64
L_RWKV = 32
SB_RWKV = 128
NSB_RWKV = 2
RWKV_VMEM_BYTES = 48 * 1024 * 1024
L_MLSTM = 256
LANE = 128

NN = (((1,), (0,)), ((), ()))
NT = (((1,), (1,)), ((), ()))
TN = (((0,), (0,)), ((), ()))


def _dot(a, b, dims=NN):
    return lax.dot_general(a.astype(BF16), b.astype(BF16), dims, preferred_element_type=F32)


def _split2(x):
    hi = x.astype(BF16)
    lo = (x - hi.astype(F32)).astype(BF16)
    return hi, lo


def _dot_sel(g, x):
    hi, lo = _split2(x)
    return (lax.dot_general(g, hi, NN, preferred_element_type=F32)
            + lax.dot_general(g, lo, NN, preferred_element_type=F32))


def _dot_sel_r(x, g):
    hi, lo = _split2(x)
    return (lax.dot_general(hi, g, NN, preferred_element_type=F32)
            + lax.dot_general(lo, g, NN, preferred_element_type=F32))


def _sigmoid(x):
    return 1.0 / (1.0 + jnp.exp(-x))


def _silu(x):
    return x * _sigmoid(x)


def _logsig(x):
    return jnp.minimum(x, 0.0) - jnp.log(1.0 + jnp.exp(-jnp.abs(x)))


def _hgrn_consts(L):
    nlev = int(np.log2(L))
    t = np.arange(L)
    G = np.zeros((nlev + 2, L, L), np.float32)
    G[0] = t[None, :] <= t[:, None]
    G[1] = t[None, :] > t[:, None]
    M = np.zeros((nlev + 1, L, L), np.float32)
    M[0] = np.eye(L)
    for lev in range(nlev):
        m = 1 << lev
        grp, pos = t // (2 * m), t % (2 * m)
        anchor = grp * 2 * m + m - 1
        for r in range(L):
            if pos[r] >= m:
                G[2 + lev, r, anchor[r] + 1:r + 1] = 1.0
            else:
                G[2 + lev, r, r + 1:anchor[r] + 1] = 1.0
        M[1 + lev] = (grp[:, None] == grp[None, :]) & (pos[:, None] >= m) & (pos[None, :] < m)
    G2 = np.stack([G.reshape(-1, L), G[:, ::-1, ::-1].reshape(-1, L)])
    M2 = np.stack([M, M[:, ::-1, ::-1]])
    return jnp.asarray(G2, BF16), jnp.asarray(M2, F32)


def _tri_consts(L):
    t = np.arange(L)
    return np.stack([t[None, :] <= t[:, None], t[None, :] >= t[:, None]]).astype(np.float32)


def _head_sum_const():
    i = np.arange(LANE)
    return jnp.asarray((i[:, None] // HS_B) == (i[None, :] // HS_B), BF16)


def _hgrn_chunks(probs, M_ref):
    L = probs[0][0].shape[0]
    nlev = M_ref.shape[1] - 1
    st = []
    for (q, v, pre, lb, St, G, d) in probs:
        f = lb + (1.0 - lb) * _sigmoid(pre)
        logf = jnp.log(f)
        st.append(dict(q=q, v=v, St=St, G=G, d=d, logf=logf, k=1.0 - f))
    for p in st:
        p["E"] = _dot_sel(p["G"], p["logf"])
    for p in st:
        p["o"] = _dot(p["q"] * jnp.exp(p["E"][0:L]), p["St"], NT)
    for p in st:
        p["att"] = M_ref[p["d"], 0] * _dot(p["q"], p["k"], NT)
    for lev in range(nlev):
        for p in st:
            F = jnp.exp(p["E"][(2 + lev) * L:(3 + lev) * L])
            p["att"] = p["att"] + M_ref[p["d"], 1 + lev] * _dot(p["q"] * F, p["k"] * F, NT)
    for p in st:
        p["o"] = p["o"] + _dot(p["att"], p["v"])
    out = []
    for p in st:
        tot = jnp.sum(p["logf"], axis=0, keepdims=True)
        St = jnp.exp(tot) * p["St"] + _dot(p["v"], p["k"] * jnp.exp(p["E"][L:2 * L]), TN)
        out.append((p["o"], St))
    return out


def _rwkv_consts(L, SB):
    t = np.arange(SB)
    same = (t[:, None] // L) == (t[None, :] // L)
    le = t[None, :] <= t[:, None]
    ge = t[None, :] >= t[:, None]
    lt = t[None, :] < t[:, None]
    gt = t[None, :] > t[:, None]
    sel = np.stack([np.concatenate([same & le, same & gt]), np.concatenate([same & ge, same & lt])])
    strict = np.stack([same & lt, same & gt])
    incl = np.stack([same & le, same & ge])
    return jnp.asarray(sel, BF16), jnp.asarray(strict, F32), jnp.asarray(incl, F32)


def _swap_heads(x):
    return pltpu.roll(x, HS_B, 1)


def _rwkv_block(L, r, lw, kd, v, vsw, kk, bv, sel, strict, incl, lm0, lm1):
    return _rwkv_blocks(L, [(r, lw, kd, v, vsw, kk, bv, sel, strict, incl)], lm0, lm1)[0]


def _rwkv_blocks(L, probs, lm0, lm1):
    SB = probs[0][0].shape[0]
    st = []
    for (r, lw, kd, v, vsw, kk, bv, sel, strict, incl) in probs:
        gg = _dot_sel(sel, lw)
        st.append(dict(r=r, lw=lw, kd=kd, v=v, vsw=vsw, kk=kk, bv=bv, strict=strict, incl=incl,
                       ginc=gg[:SB], gend=gg[SB:]))
    for p in st:
        e_inc = jnp.exp(p["ginc"])
        e_exc = jnp.exp(p["ginc"] - p["lw"])
        e_neg = jnp.exp(-p["ginc"])
        e_end = jnp.exp(p["gend"])
        p["Kq"], p["Rq"] = p["kk"] * e_exc, p["r"] * e_inc
        Bi, Ki = p["bv"] * e_neg, p["kd"] * e_neg
        p["Be"], p["Ke"] = p["bv"] * e_end, p["kd"] * e_end
        p["lhs"] = jnp.concatenate([p["Kq"] * lm0, p["Kq"] * lm1, p["Rq"] * lm0, p["Rq"] * lm1], axis=0)
        p["rhs"] = jnp.concatenate([Bi, Ki], axis=0)
    for p in st:
        p["sc"] = _dot(p["lhs"], p["rhs"], NT)
    ch = []
    for p in st:
        for h in (0, 1):
            sc = p["sc"]
            ch.append(dict(p=p, h=h,
                           Mb=p["strict"] * sc[h * SB:(h + 1) * SB, :SB],
                           Mk=p["strict"] * sc[h * SB:(h + 1) * SB, SB:],
                           Nb=p["incl"] * sc[(2 + h) * SB:(3 + h) * SB, :SB],
                           Nk=p["incl"] * sc[(2 + h) * SB:(3 + h) * SB, SB:]))
    for c in ch:
        c["MkV"] = _dot(c["Mk"], c["p"]["vsw"])
    for c in ch:
        lmh, lmo = (lm0, lm1) if c["h"] == 0 else (lm1, lm0)
        c["X"] = lmh * c["p"]["Kq"] + lmo * c["MkV"]
        c["Mp"] = c["Mb"]
    for c in ch:
        c["X"] = c["X"] - _dot(c["Mb"], c["X"])
    n = 2
    while n < L:
        for c in ch:
            c["Mp"] = _dot(c["Mp"], c["Mp"])
        for c in ch:
            c["X"] = c["X"] + _dot(c["Mp"], c["X"])
        n *= 2
    for c in ch:
        c["G"] = _dot(c["Nb"], c["X"])
    for c in ch:
        c["NkV"] = _dot(c["Nk"], c["p"]["v"])
    out = []
    for i, p in enumerate(st):
        c0, c1 = ch[2 * i], ch[2 * i + 1]
        A1 = lm0 * c0["X"] + lm1 * c1["X"]
        A2 = _swap_heads(lm0 * c1["X"] + lm1 * c0["X"])
        Qe = p["Rq"] - (lm0 * c0["G"] + lm1 * c1["G"])
        Yl = lm0 * c0["NkV"] + lm1 * c1["NkV"] - _swap_heads(lm0 * c1["G"] + lm1 * c0["G"])
        out.append((Qe, Yl, A1, A2, p["Be"], p["Ke"], p["ginc"] + p["gend"]))
    return out


def _mlstm_chunk(q, k, v, gi_col, gf_col, gi_row, gf_row, C, n, m, tri, tri_t, causal):
    L = q.shape[0]
    lf_col = _logsig(gf_col)
    lf_row = _logsig(gf_row)
    b_col = _dot_sel(tri, jnp.broadcast_to(lf_col, (L, LANE)))[:, 0:1]
    b_row = _dot_sel_r(jnp.broadcast_to(lf_row, (8, L)), tri_t)[0:1, :]
    btot = jnp.sum(lf_col, axis=0, keepdims=True)
    log_d = jnp.where(causal > 0, b_col - b_row + gi_row, -jnp.inf)
    m_loc = jnp.max(log_d, axis=1, keepdims=True)
    log_prev = b_col + m
    m_t = jnp.maximum(log_prev, m_loc)
    w_prev = jnp.exp(log_prev - m_t)
    s = _dot(q, k, NT) * jnp.exp(log_d - m_t)
    num = w_prev * _dot(q, C) + _dot(s, v)
    den = w_prev * jnp.sum(q * n, axis=1, keepdims=True) + jnp.sum(s, axis=1, keepdims=True)
    h = num / jnp.maximum(jnp.abs(den), jnp.exp(-m_t))
    log_s = btot - b_col + gi_col
    m_new = jnp.maximum(btot + m, jnp.max(log_s, axis=0, keepdims=True))
    kw = k * jnp.exp(log_s - m_new)
    w_old = jnp.exp(btot + m - m_new)
    C = w_old * C + _dot(kw, v, TN)
    n = w_old * n + jnp.sum(kw, axis=0, keepdims=True)
    return h, C, n, m_new


def _adaln_kernel(c_ref, w_ref, b_ref, o_ref):
    o_ref[0] = _dot(_silu(c_ref[...]), w_ref[0]) + b_ref[0]


def _adaln(cond8, w_mod, b_mod):
    depth, _, n3 = w_mod.shape
    tn = 1024
    return pl.pallas_call(
        _adaln_kernel,
        grid=(depth, n3 // tn),
        in_specs=[pl.BlockSpec((8, D_MODEL), lambda l, j: (0, 0)),
                  pl.BlockSpec((1, D_MODEL, tn), lambda l, j: (l, 0, j)),
                  pl.BlockSpec((1, 1, tn), lambda l, j: (l, 0, j))],
        out_specs=pl.BlockSpec((1, 8, tn), lambda l, j: (l, 0, j)),
        out_shape=jax.ShapeDtypeStruct((depth, 8, n3), F32),
        name="adaln",
    )(cond8, w_mod, b_mod.reshape(depth, 1, n3))


def _rms_mod(x, g, mod):
    y = x * lax.rsqrt(jnp.mean(x * x, axis=-1, keepdims=True) + EPS) * g
    return y * (1.0 + mod[:, D_MODEL:2 * D_MODEL]) + mod[:, 0:D_MODEL]


def _norm_mod_kernel(x_ref, g_ref, mod_ref, h_ref):
    h_ref[...] = _rms_mod(x_ref[...], g_ref[...], mod_ref[0]).astype(BF16)


def _norm_mod(x, g, mod_rows):
    nblk = x.shape[0] // ROW_BLK
    return pl.pallas_call(
        _norm_mod_kernel,
        grid=(nblk,),
        in_specs=[pl.BlockSpec((ROW_BLK, D_MODEL), lambda i: (i, 0)),
                  pl.BlockSpec((1, D_MODEL), lambda i: (0, 0)),
                  pl.BlockSpec((1, 1, 3 * D_MODEL), lambda i: (i, 0, 0))],
        out_specs=pl.BlockSpec((ROW_BLK, D_MODEL), lambda i: (i, 0)),
        out_shape=jax.ShapeDtypeStruct(x.shape, BF16),
        name="norm_mod",
    )(x, g, mod_rows)


def _matmul_kernel(a_ref, w_ref, o_ref):
    o_ref[...] = jnp.dot(a_ref[...], w_ref[...], preferred_element_type=F32)


def _matmul(a, w, tm=1024, tn=512):
    m, k = a.shape
    n = w.shape[1]
    tn = min(tn, n)
    tiles = 2 * (tm * k * 2 + k * tn * 2 + tm * tn * 4)
    return pl.pallas_call(
        _matmul_kernel,
        grid=(m // tm, n // tn),
        in_specs=[pl.BlockSpec((tm, k), lambda i, j: (i, 0)),
                  pl.BlockSpec((k, tn), lambda i, j: (0, j))],
        out_specs=pl.BlockSpec((tm, tn), lambda i, j: (i, j)),
        out_shape=jax.ShapeDtypeStruct((m, n), F32),
        compiler_params=pltpu.CompilerParams(vmem_limit_bytes=tiles + tiles // 4),
        name="in_proj",
    )(a, w)


def _add_prev(in_specs, args, prev):
    if prev is None:
        return {}
    in_specs.append(pl.BlockSpec(memory_space=pl.ANY))
    args.append(prev)
    return {len(args) - 1: 0}


def _hgrn_kernel(T, has_state, has_prev, want_final, *refs):
    q_ref, v_ref, ff_ref, fb_ref, z_ref, lb_ref, gn_ref, G_ref, M_ref = refs[:9]
    pos = 9
    s0_ref = None
    if has_state:
        s0_ref = refs[pos]
        pos += 1
    pos += int(has_prev)
    out_ref = refs[pos]
    pos += 1
    sfin_ref = None
    if want_final:
        sfin_ref = refs[pos]
        pos += 1
    o_dir = refs[pos]
    L = L_HGRN
    nc = T // L
    lb = lb_ref[0]
    f_refs = (ff_ref, fb_ref)

    def step(c, carry):
        probs = []
        for d in (0, 1):
            ci = c if d == 0 else nc - 1 - c
            rows = pl.ds(pl.multiple_of(ci * L, L), L)
            probs.append((q_ref[rows, :], v_ref[rows, :], f_refs[d][rows, :], lb, carry[d], G_ref[d], d))
        res = _hgrn_chunks(probs, M_ref)
        for d in (0, 1):
            ci = c if d == 0 else nc - 1 - c
            o_dir[d, pl.ds(pl.multiple_of(ci * L, L), L), :] = res[d][0]
        return (res[0][1], res[1][1])

    if has_state:
        init = (s0_ref[0, 0, 0], s0_ref[0, 1, 0])
    else:
        init = (jnp.zeros((DK_A, DK_A), F32), jnp.zeros((DK_A, DK_A), F32))
    fin = lax.fori_loop(0, nc, step, init)
    if want_final:
        sfin_ref[0, 0, 0] = fin[0]
        sfin_ref[0, 1, 0] = fin[1]
    o = o_dir[0] + o_dir[1]
    y = o * lax.rsqrt(jnp.mean(o * o, axis=-1, keepdims=True) + EPS) * gn_ref[0]
    out_ref[...] = (y * _silu(z_ref[...])).astype(BF16)


def _hgrn(proj, lb, gn, consts, prev_out, nb, T, row0, s0_t=None, want_final=False):
    G, M = consts
    rb0 = row0 // T
    col = lambda off: pl.BlockSpec((T, LANE), lambda b, h, off=off: (rb0 + b, off + h))
    in_specs = [col(0), col(8), col(16), col(24), col(32),
                pl.BlockSpec((1, 1, LANE), lambda b, h: (h, 0, 0)),
                pl.BlockSpec((1, 1, LANE), lambda b, h: (h, 0, 0)),
                pl.BlockSpec(G.shape, lambda b, h: (0, 0, 0)),
                pl.BlockSpec(M.shape, lambda b, h: (0, 0, 0, 0))]
    args = [proj, proj, proj, proj, proj, lb, gn, G, M]
    if s0_t is not None:
        in_specs.append(pl.BlockSpec((1, 2, 1, DK_A, DK_A), lambda b, h: (b, 0, h, 0, 0)))
        args.append(s0_t)
    aliases = _add_prev(in_specs, args, prev_out)
    out_specs = [pl.BlockSpec((T, LANE), lambda b, h: (rb0 + b, h))]
    out_shape = [jax.ShapeDtypeStruct((N_TOK, D_A), BF16)]
    if want_final:
        out_specs.append(pl.BlockSpec((1, 2, 1, DK_A, DK_A), lambda b, h: (b, 0, h, 0, 0)))
        out_shape.append(jax.ShapeDtypeStruct((nb, 2, H_A, DK_A, DK_A), F32))
    res = pl.pallas_call(
        functools.partial(_hgrn_kernel, T, s0_t is not None, prev_out is not None, want_final),
        grid=(nb, H_A),
        in_specs=in_specs,
        out_specs=out_specs,
        out_shape=out_shape,
        scratch_shapes=[pltpu.VMEM((2, T, LANE), F32)],
        input_output_aliases=aliases,
        name="hgrn2_T%d" % T,
    )(*args)
    return res


def _seq_pos(shape):
    is_latent = pl.program_id(0) >= N_P // SEQ_BLK
    seq_len = jnp.where(is_latent, DEC_SEQ, SEQ)
    t = lax.broadcasted_iota(jnp.int32, shape, 0)
    return jnp.bitwise_and(t, seq_len - 1), seq_len, is_latent


def _shift_kernel(p_ref, mu_ref, o_ref):
    p = p_ref[...]
    pos, seq_len, _ = _seq_pos(p.shape)
    prev = jnp.where(pos == 0, 0.0, pltpu.roll(p, 1, 0))
    nxt = jnp.where(pos == seq_len - 1, 0.0, pltpu.roll(p, p.shape[0] - 1, 0))
    mu = mu_ref[...]
    o_ref[...] = p + mu[0:1] * (prev - p) + mu[1:2] * (nxt - p)


def _shift(proj, mu):
    tc = 256
    c0 = (5 * D_A) // tc
    return pl.pallas_call(
        _shift_kernel,
        grid=(N_TOK // SEQ_BLK, SHIFT_W_B // tc),
        in_specs=[pl.BlockSpec((SEQ_BLK, tc), lambda i, j: (i, c0 + j)),
                  pl.BlockSpec((2, tc), lambda i, j: (0, j))],
        out_specs=pl.BlockSpec((SEQ_BLK, tc), lambda i, j: (i, j)),
        out_shape=jax.ShapeDtypeStruct((N_TOK, SHIFT_W_B), F32),
        name="rwkv_shift",
    )(proj, mu)


def _rwkv_prep_kernel(r_ref, k_ref, v_ref, low_ref, w0_ref, w2_ref, a0_ref, a2_ref, kk_ref_p, ka_ref, rk_ref,
                      bd_ref, kk_ref, lw_ref, kd_ref, bv_ref, bonus_ref):
    r, k, v, low = r_ref[...], k_ref[...], v_ref[...], low_ref[...]
    bd = bd_ref[...]
    tl = jnp.tanh(low)
    kkr = k * kk_ref_p[...]
    kk = kkr / jnp.maximum(jnp.sqrt(_dot_sel_r(kkr * kkr, bd)), 1e-12)
    kk_ref[...] = kk
    bonus = jnp.zeros_like(r)
    for d in (0, 1):
        u = w0_ref[d] + _dot(tl, w2_ref[d])
        lw_ref[d] = -np.float32(np.exp(-0.5)) * _sigmoid(u)
        a = _sigmoid(a0_ref[d] + _dot(low, a2_ref[d]))
        kd = k * (1.0 + (a - 1.0) * ka_ref[...])
        kd_ref[d] = kd
        bv_ref[d] = a * kk
        bonus = bonus + _dot_sel_r(r * kd * rk_ref[...], bd) * v
    bonus_ref[...] = bonus


def _rwkv_prep(sh, w0, w2p, a0, a2p, k_k, k_a, r_k, bd, tm=1024):
    n = sh.shape[0]
    tile = lambda off: pl.BlockSpec((tm, LANE), lambda i, j, off=off: (i, off + j))
    vec = pl.BlockSpec((1, LANE), lambda i, j: (0, j))
    dvec = pl.BlockSpec((2, 1, LANE), lambda i, j: (0, 0, j))
    dmat = pl.BlockSpec((2, 4 * R_LOW, LANE), lambda i, j: (0, 0, j))
    dout = pl.BlockSpec((2, tm, LANE), lambda i, j: (0, i, j))
    one = jax.ShapeDtypeStruct((n, D_B), F32)
    two = jax.ShapeDtypeStruct((2, n, D_B), F32)
    return pl.pallas_call(
        _rwkv_prep_kernel,
        grid=(n // tm, D_B // LANE),
        in_specs=[tile(0), tile(8), tile(16),
                  pl.BlockSpec((tm, 4 * R_LOW), lambda i, j: (i, (3 * D_B) // (4 * R_LOW))),
                  dvec, dmat, dvec, dmat, vec, vec, vec,
                  pl.BlockSpec((LANE, LANE), lambda i, j: (0, 0))],
        out_specs=[tile(0), dout, dout, dout, tile(0)],
        out_shape=[one, two, two, two, one],
        name="rwkv_prep",
    )(sh, sh, sh, sh, w0, w2p, a0, a2p, k_k, k_a, r_k, bd)


def _rwkv_kernel(T, has_state, has_prev, want_final, *refs):
    (r_ref, v_ref, kk_ref, lw_ref, kd_ref, bv_ref, sel_ref, strict_ref, incl_ref, bdeye_ref) = refs[:10]
    pos = 10
    s0_ref = None
    if has_state:
        s0_ref = refs[pos]
        pos += 1
    pos += int(has_prev)
    y_ref = refs[pos]
    pos += 1
    sfin_ref = None
    if want_final:
        sfin_ref = refs[pos]
        pos += 1
    tm_scr, u_scr, qe_scr, s_scr = refs[pos:pos + 4]
    L, SB = L_RWKV, SB_RWKV
    nc = T // L
    cps = SB // L

    def block_step(it, _):
        lane = lax.broadcasted_iota(jnp.int32, (1, LANE), 1)
        lm0 = (lane < HS_B).astype(F32)
        lm1 = 1.0 - lm0
        probs, keys, vs = [], [], {}
        for s in range(NSB_RWKV):
            sb = it * NSB_RWKV + s
            rows = pl.ds(pl.multiple_of(sb * SB, SB), SB)
            r, v, kk = r_ref[rows, :], v_ref[rows, :], kk_ref[rows, :]
            vsw = _swap_heads(v)
            vs[s] = v
            for d in (0, 1):
                probs.append((r, lw_ref[d, rows, :], kd_ref[d, rows, :], v, vsw, kk, bv_ref[d, rows, :],
                              sel_ref[d], strict_ref[d], incl_ref[d]))
                keys.append((s, sb, rows, d))
        res = _rwkv_blocks(L, probs, lm0, lm1)
        for (s, sb, rows, d), out in zip(keys, res):
            y_ref[d, rows, :] = out[1]
            qe_scr[d, rows, :] = out[0].astype(BF16)
        for j in range(cps):
            rc = slice(j * L, (j + 1) * L)
            for (s, sb, rows, d), (_, _, A1, A2, Be, Ke, gt) in zip(keys, res):
                decay = bdeye_ref[1] * jnp.exp(gt[j * L:j * L + 1])
                tm_scr[d, sb * cps + j] = (decay - bdeye_ref[0] * _dot(A1[rc], Be[rc], TN)).astype(BF16)
            for (s, sb, rows, d), (_, _, A1, A2, Be, Ke, gt) in zip(keys, res):
                u_scr[d, sb * cps + j] = bdeye_ref[0] * _dot(
                    jnp.concatenate([vs[s][rc], -A2[rc]], axis=0), jnp.concatenate([Ke[rc], Be[rc]], axis=0), TN)
        return 0

    lax.fori_loop(0, T // (SB * NSB_RWKV), block_step, 0)

    def chunk_step(c, _):
        for d in (0, 1):
            ci = c if d == 0 else nc - 1 - c
            rows = pl.ds(pl.multiple_of(ci * L, L), L)
            S = s_scr[d].astype(BF16)
            y_ref[d, rows, :] = y_ref[d, rows, :] + lax.dot_general(qe_scr[d, rows, :], S, NT,
                                                                    preferred_element_type=F32)
            s_scr[d] = lax.dot_general(S, tm_scr[d, ci], NN, preferred_element_type=F32) + u_scr[d, ci]
        return 0

    zero = jnp.zeros((HS_B, HS_B), F32)
    for d in (0, 1):
        if has_state:
            top = jnp.concatenate([s0_ref[0, d, 0], zero], axis=1)
            bot = jnp.concatenate([zero, s0_ref[0, d, 1]], axis=1)
            s_scr[d] = jnp.concatenate([top, bot], axis=0)
        else:
            s_scr[d] = jnp.zeros((LANE, LANE), F32)
    lax.fori_loop(0, nc, chunk_step, 0)
    if want_final:
        for d in (0, 1):
            fin = s_scr[d]
            sfin_ref[0, d, 0] = fin[0:HS_B, 0:HS_B]
            sfin_ref[0, d, 1] = fin[HS_B:LANE, HS_B:LANE]


def _rwkv(sh, kk, lw, kd, bv, consts, prev_y, nb, T, row0, s0=None, want_final=False):
    sel, strict, incl, bdeye = consts
    rb0 = row0 // T
    nhp = H_B // 2
    nc = T // L_RWKV
    tok = lambda off: pl.BlockSpec((T, LANE), lambda b, p, off=off: (rb0 + b, off + p))
    dtok = pl.BlockSpec((2, T, LANE), lambda b, p: (0, rb0 + b, p))
    cst = lambda a: pl.BlockSpec(a.shape, lambda b, p: (0, 0, 0))
    st = pl.BlockSpec((1, 2, 2, HS_B, HS_B), lambda b, p: (b, 0, p, 0, 0))
    in_specs = [tok(0), tok(2 * D_B // LANE), tok(0), dtok, dtok, dtok, cst(sel), cst(strict), cst(incl), cst(bdeye)]
    args = [sh, sh, kk, lw, kd, bv, sel, strict, incl, bdeye]
    if s0 is not None:
        in_specs.append(st)
        args.append(s0)
    aliases = _add_prev(in_specs, args, prev_y)
    out_specs = [dtok]
    out_shape = [jax.ShapeDtypeStruct((2, N_TOK, D_B), F32)]
    if want_final:
        out_specs.append(st)
        out_shape.append(jax.ShapeDtypeStruct((nb, 2, H_B, HS_B, HS_B), F32))
    return pl.pallas_call(
        functools.partial(_rwkv_kernel, T, s0 is not None, prev_y is not None, want_final),
        grid=(nb, nhp),
        in_specs=in_specs,
        out_specs=out_specs,
        out_shape=out_shape,
        scratch_shapes=[pltpu.VMEM((2, nc, LANE, LANE), BF16), pltpu.VMEM((2, nc, LANE, LANE), F32),
                        pltpu.VMEM((2, T, LANE), BF16), pltpu.VMEM((2, LANE, LANE), F32)],
        input_output_aliases=aliases,
        compiler_params=pltpu.CompilerParams(vmem_limit_bytes=RWKV_VMEM_BYTES),
        name="rwkv7_T%d" % T,
    )(*args)


def _rwkv_combine_kernel(y_ref, bonus_ref, z_ref, g_ref, b_ref, bd_ref, o_ref):
    bd = bd_ref[...]
    y = y_ref[0] + y_ref[1]
    yc = y - _dot_sel_r(y, bd) * (1.0 / HS_B)
    var = _dot_sel_r(yc * yc, bd) * (1.0 / HS_B)
    yn = yc * lax.rsqrt(var + GN_EPS_B) * g_ref[...] + b_ref[...]
    o_ref[...] = ((yn + bonus_ref[...]) * _silu(z_ref[...])).astype(BF16)


def _rwkv_combine(y2, bonus, proj, gn_g, gn_b, bd, tm=1024):
    n = bonus.shape[0]
    zc0 = (5 * D_A + SHIFT_W_B) // LANE
    tile = pl.BlockSpec((tm, LANE), lambda i, j: (i, j))
    vec = pl.BlockSpec((1, LANE), lambda i, j: (0, j))
    return pl.pallas_call(
        _rwkv_combine_kernel,
        grid=(n // tm, D_B // LANE),
        in_specs=[pl.BlockSpec((2, tm, LANE), lambda i, j: (0, i, j)), tile,
                  pl.BlockSpec((tm, LANE), lambda i, j: (i, zc0 + j)), vec, vec,
                  pl.BlockSpec((LANE, LANE), lambda i, j: (0, 0))],
        out_specs=tile,
        out_shape=jax.ShapeDtypeStruct((n, D_B), BF16),
        name="rwkv_combine",
    )(y2, bonus, proj, gn_g, gn_b, bd)


def _out_proj_kernel(final, a_ref, b_ref, w_ref, x_ref, mod_ref, g_ref, *rest):
    acc = jnp.dot(a_ref[...], w_ref[0:D_MODEL, :], preferred_element_type=F32)
    acc = acc + jnp.dot(b_ref[...], w_ref[D_MODEL:2 * D_MODEL, :], preferred_element_type=F32)
    x = x_ref[...] + mod_ref[0][:, 2 * D_MODEL:3 * D_MODEL] * acc
    if final:
        (y_ref,) = rest
        y_ref[...] = x * lax.rsqrt(jnp.mean(x * x, axis=-1, keepdims=True) + EPS) * g_ref[...]
    else:
        nmod_ref, x_out_ref, h_ref = rest
        x_out_ref[...] = x
        h_ref[...] = _rms_mod(x, g_ref[...], nmod_ref[0]).astype(BF16)


def _out_proj(a, b, a_col, b_col, w, x, mod_rows, g, next_mod_rows=None):
    n = x.shape[0]
    nblk = n // ROW_BLK
    final = next_mod_rows is None
    row = lambda c: pl.BlockSpec((ROW_BLK, D_MODEL), lambda i, c=c: (i, c))
    modspec = pl.BlockSpec((1, 1, 3 * D_MODEL), lambda i: (i, 0, 0))
    in_specs = [row(a_col), row(b_col), pl.BlockSpec(w.shape, lambda i: (0, 0)), row(0), modspec,
                pl.BlockSpec((1, D_MODEL), lambda i: (0, 0))]
    args = [a, b, w, x, mod_rows, g]
    if final:
        out_specs = row(0)
        out_shape = jax.ShapeDtypeStruct((n, D_MODEL), F32)
    else:
        in_specs.append(modspec)
        args.append(next_mod_rows)
        out_specs = [row(0), row(0)]
        out_shape = [jax.ShapeDtypeStruct((n, D_MODEL), F32), jax.ShapeDtypeStruct((n, D_MODEL), BF16)]
    return pl.pallas_call(
        functools.partial(_out_proj_kernel, final),
        grid=(nblk,),
        in_specs=in_specs,
        out_specs=out_specs,
        out_shape=out_shape,
        name="out_proj_final" if final else "out_proj",
    )(*args)


def _conv_taps(x_ref, w_ref, b_ref, o_ref, two_d, seq_len):
    n, tc = x_ref.shape
    t = lax.broadcasted_iota(jnp.int32, (n, LANE), 0)
    pos = jnp.bitwise_and(t, seq_len - 1)
    line = GRID_W if two_d else seq_len
    col = jnp.bitwise_and(pos, line - 1)
    not_last = (col != line - 1).astype(F32)
    not_first = (col != 0).astype(F32)
    if two_d:
        r = jnp.right_shift(pos, int(np.log2(GRID_W)))
        has_up = (r != 0).astype(F32)
        has_down = (r != seq_len // GRID_W - 1).astype(F32)
    for j in range(tc // LANE):
        cs = slice(j * LANE, (j + 1) * LANE)
        x = x_ref[:, cs]
        left = pltpu.roll(x * not_last, 1, 0)
        right = pltpu.roll(x * not_first, n - 1, 0)
        wrow = lambda i: (w_ref[3 * i:3 * i + 1, cs] * left + w_ref[3 * i + 1:3 * i + 2, cs] * x
                          + w_ref[3 * i + 2:3 * i + 3, cs] * right)
        acc = wrow(1) + b_ref[:, cs]
        if two_d:
            acc = acc + has_up * pltpu.roll(wrow(0), GRID_W, 0) + has_down * pltpu.roll(wrow(2), n - GRID_W, 0)
        o_ref[:, cs] = _silu(acc)


def _conv_kernel(x_ref, w_ref, b_ref, o_ref):
    is_latent = pl.program_id(0) >= N_P // SEQ_BLK

    @pl.when(jnp.logical_not(is_latent))
    def _():
        _conv_taps(x_ref, w_ref, b_ref, o_ref, False, SEQ)

    @pl.when(is_latent)
    def _():
        _conv_taps(x_ref, w_ref, b_ref, o_ref, True, DEC_SEQ)


def _conv(proj, w9, bias, tc=256):
    return pl.pallas_call(
        _conv_kernel,
        grid=(N_TOK // SEQ_BLK, (2 * D_C) // tc),
        in_specs=[pl.BlockSpec((SEQ_BLK, tc), lambda i, j: (i, j)),
                  pl.BlockSpec((9, tc), lambda i, j: (0, j)),
                  pl.BlockSpec((1, tc), lambda i, j: (0, j))],
        out_specs=pl.BlockSpec((SEQ_BLK, tc), lambda i, j: (i, j)),
        out_shape=jax.ShapeDtypeStruct((N_TOK, 2 * D_C), F32),
        name="mlstm_conv",
    )(proj, w9, bias)


def _mlstm_kernel(nc, has_state, has_prev, want_final, *refs):
    (q_ref, k_ref, v_ref, ic_ref, fc_ref, ir_ref, fr_ref, bi_ref, bf_ref,
     tri_ref, trit_ref, causal_ref) = refs[:12]
    pos = 12
    if has_state:
        c0_ref, n0_ref, m0_ref = refs[pos:pos + 3]
        pos += 3
    pos += int(has_prev)
    h_ref = refs[pos]
    pos += 1
    if want_final:
        cf_ref, nf_ref, mf_ref = refs[pos:pos + 3]
        pos += 3
    c_scr, n_scr, m_scr = refs[pos:pos + 3]
    c = pl.program_id(3)

    @pl.when(c == 0)
    def _():
        if has_state:
            c_scr[...] = c0_ref[0, 0, 0]
            n_scr[...] = n0_ref[0, 0, 0]
            m_scr[...] = m0_ref[0, 0, 0]
        else:
            c_scr[...] = jnp.zeros_like(c_scr)
            n_scr[...] = jnp.zeros_like(n_scr)
            m_scr[...] = jnp.zeros_like(m_scr)

    bi, bf = bi_ref[0, 0], bf_ref[0, 0]
    k = k_ref[...] * np.float32(DH_C ** -0.5)
    h, C, n, m = _mlstm_chunk(q_ref[...], k, v_ref[...],
                              ic_ref[0, 0] + bi, fc_ref[0, 0] + bf, ir_ref[0, 0] + bi, fr_ref[0, 0] + bf,
                              c_scr[...], n_scr[...], m_scr[...],
                              tri_ref[0], trit_ref[0], causal_ref[0])
    h_ref[0] = h
    c_scr[...] = C
    n_scr[...] = n
    m_scr[...] = m
    if want_final:
        @pl.when(c == nc - 1)
        def _():
            cf_ref[0, 0, 0] = C
            nf_ref[0, 0, 0] = n
            mf_ref[0, 0, 0] = m


def _mlstm(qk, proj, gates, consts, prev_h, nb, T, row0, state=None, want_final=False):
    ic, fc, ir, fr, bi, bf = gates
    tri, trit, causal = consts
    L = L_MLSTM
    nc = T // L
    rb0 = row0 // L
    blk = lambda b, d, c: rb0 + b * nc + c + d * (nc - 1 - 2 * c)
    tok = lambda off: pl.BlockSpec((L, DH_C), lambda b, h, d, c, off=off: (blk(b, d, c), off + h))
    gcol = pl.BlockSpec((1, 1, L, 1), lambda b, h, d, c: (d, h, blk(b, d, c), 0))
    grow = pl.BlockSpec((1, 1, 1, L), lambda b, h, d, c: (d, h, 0, blk(b, d, c)))
    gb = pl.BlockSpec((1, 1, 1, 1), lambda b, h, d, c: (d, h, 0, 0))
    cst = pl.BlockSpec((1, L, L), lambda b, h, d, c: (d, 0, 0))
    cs = pl.BlockSpec((1, 1, 1, DH_C, DH_C), lambda b, h, d, c: (b, d, h, 0, 0))
    ns = pl.BlockSpec((1, 1, 1, 1, DH_C), lambda b, h, d, c: (b, d, h, 0, 0))
    ms = pl.BlockSpec((1, 1, 1, 1, 1), lambda b, h, d, c: (b, d, h, 0, 0))
    in_specs = [tok(0), tok(H_C), tok(2 * H_C), gcol, gcol, grow, grow, gb, gb, cst, cst, cst]
    args = [qk, qk, proj, ic, fc, ir, fr, bi, bf, tri, trit, causal]
    if state is not None:
        in_specs += [cs, ns, ms]
        args += list(state)
    aliases = _add_prev(in_specs, args, prev_h)
    out_specs = [pl.BlockSpec((1, L, DH_C), lambda b, h, d, c: (d, blk(b, d, c), h))]
    out_shape = [jax.ShapeDtypeStruct((2, N_TOK, D_C), F32)]
    if want_final:
        out_specs += [cs, ns, ms]
        out_shape += [jax.ShapeDtypeStruct((nb, 2, H_C, DH_C, DH_C), F32),
                      jax.ShapeDtypeStruct((nb, 2, H_C, 1, DH_C), F32),
                      jax.ShapeDtypeStruct((nb, 2, H_C, 1, 1), F32)]
    return pl.pallas_call(
        functools.partial(_mlstm_kernel, nc, state is not None, prev_h is not None, want_final),
        grid=(nb, H_C, 2, nc),
        in_specs=in_specs,
        out_specs=out_specs,
        out_shape=out_shape,
        scratch_shapes=[pltpu.VMEM((DH_C, DH_C), F32), pltpu.VMEM((1, DH_C), F32), pltpu.VMEM((1, 1), F32)],
        input_output_aliases=aliases,
        name="mlstm_T%d" % T,
    )(*args)


def _mlstm_combine_kernel(h_ref, o_ref, z_ref, g_ref, y_ref):
    y = _sigmoid(o_ref[...]) * (h_ref[0] + h_ref[1])
    y = y * lax.rsqrt(jnp.mean(y * y, axis=-1, keepdims=True) + EPS) * g_ref[...]
    y_ref[...] = (y * _silu(z_ref[...])).astype(BF16)


def _mlstm_combine(h2, proj, norm_g, tm=512):
    n = proj.shape[0]
    return pl.pallas_call(
        _mlstm_combine_kernel,
        grid=(n // tm, H_C),
        in_specs=[pl.BlockSpec((2, tm, DH_C), lambda i, h: (0, i, h)),
                  pl.BlockSpec((tm, DH_C), lambda i, h: (i, 3 * H_C + h)),
                  pl.BlockSpec((tm, DH_C), lambda i, h: (i, 4 * H_C + h)),
                  pl.BlockSpec((1, DH_C), lambda i, h: (0, h))],
        out_specs=pl.BlockSpec((tm, DH_C), lambda i, h: (i, h)),
        out_shape=jax.ShapeDtypeStruct((n, D_C), BF16),
        name="mlstm_combine",
    )(h2, proj, proj, norm_g)


def kernel(x_prompt, x_sample, c, state_hgrn, state_rwkv, state_mlstm_C, state_mlstm_n, state_mlstm_m, c_ctx, w_mod, b_mod, norm_g, final_norm_g, w_in_even, w_out_even, hgrn_lb_logits, hgrn_norm_g, rwkv_shift_mu, rwkv_w0, rwkv_w2, rwkv_a0, rwkv_a2, rwkv_k_k, rwkv_k_a, rwkv_r_k, rwkv_gn_g, rwkv_gn_b, w_in_odd, w_out_odd, mlstm_conv_w, mlstm_conv_b, mlstm_gate_b, mlstm_norm_g):
    dt = x_prompt.dtype
    x = jnp.concatenate([x_prompt.reshape(N_P, D_MODEL), x_sample.reshape(N_S, D_MODEL)], axis=0)

    cond8 = jnp.zeros((8, D_MODEL), F32).at[0].set(c_ctx).at[1:1 + DEC_BATCH].set(c)
    mods = _adaln(cond8, w_mod, b_mod)
    blk_cond = np.concatenate([np.zeros(N_P // ROW_BLK, np.int32),
                               1 + np.repeat(np.arange(DEC_BATCH, dtype=np.int32), DEC_SEQ // ROW_BLK)])
    mod_rows = [mods[l][blk_cond][:, None, :] for l in range(2)]

    h0 = _norm_mod(x, norm_g[0:1], mod_rows[0])
    w_in0 = jnp.pad(w_in_even[0].astype(BF16), ((0, 0), (0, IN_EVEN_PAD - IN_EVEN)))
    proj = _matmul(h0, w_in0, tn=IN_EVEN_PAD // 4)

    lb_all = jnp.cumsum(jax.nn.softmax(hgrn_lb_logits.astype(F32), axis=0), axis=0)
    lb = lb_all[0].reshape(H_A, 1, DK_A)
    gn_a = hgrn_norm_g[0].reshape(H_A, 1, DK_A)
    hconsts = _hgrn_consts(L_HGRN)
    s_hgrn_t = jnp.swapaxes(state_hgrn[:, 0], -1, -2)
    out_a, sfin_a = _hgrn(proj, lb, gn_a, hconsts, None, BATCH, SEQ, 0, want_final=True)
    (out_a,) = _hgrn(proj, lb, gn_a, hconsts, out_a, DEC_BATCH, DEC_SEQ, N_P, s0_t=s_hgrn_t)
    new_hgrn = jnp.swapaxes(sfin_a, -1, -2)[:, None].astype(dt)

    mu = rwkv_shift_mu[0]
    sh = _shift(proj, mu)
    low_pad = lambda w, off: jnp.zeros((2, 4 * R_LOW, D_B), F32).at[0, off:off + R_LOW].set(w[0]).at[
        1, off + R_LOW:off + 2 * R_LOW].set(w[1]).astype(BF16)
    bd = _head_sum_const()
    kk, lw, kd, bv, bonus = _rwkv_prep(
        sh, rwkv_w0[0].reshape(2, 1, D_B), low_pad(rwkv_w2[0], 0), rwkv_a0[0].reshape(2, 1, D_B),
        low_pad(rwkv_a2[0], 2 * R_LOW), rwkv_k_k[0].reshape(1, D_B), rwkv_k_a[0].reshape(1, D_B),
        rwkv_r_k[0].reshape(1, D_B), bd)
    lane_head = np.arange(LANE) // HS_B
    bdeye = jnp.asarray(np.stack([lane_head[:, None] == lane_head[None, :], np.eye(LANE, dtype=bool)]), F32)
    rconsts = _rwkv_consts(L_RWKV, SB_RWKV) + (bdeye,)
    y2, sfin_b = _rwkv(sh, kk, lw, kd, bv, rconsts, None, BATCH, SEQ, 0, want_final=True)
    (y2,) = _rwkv(sh, kk, lw, kd, bv, rconsts, y2, DEC_BATCH, DEC_SEQ, N_P, s0=state_rwkv[:, 0])
    new_rwkv = sfin_b[:, None].astype(dt)
    out_b = _rwkv_combine(y2, bonus, proj, rwkv_gn_g[0].reshape(1, D_B), rwkv_gn_b[0].reshape(1, D_B), bd)

    x1, h1 = _out_proj(out_a, out_b, 0, 0, w_out_even[0].astype(BF16), x, mod_rows[0], norm_g[1:2],
                       next_mod_rows=mod_rows[1])

    w_in1 = w_in_odd[0].astype(BF16)
    proj1 = _matmul(h1, w_in1[:, :IN_ODD_MAIN], tn=IN_ODD_MAIN // 4)
    w_gate = jnp.pad(w_in1[:, IN_ODD_MAIN:], ((0, 0), (0, LANE - 4 * H_C)))
    gates = _matmul(h1, w_gate)[:, :4 * H_C]
    g4 = gates.reshape(N_TOK, 4, H_C).transpose(1, 2, 0)
    gate_args = (g4[0:2, :, :, None], g4[2:4, :, :, None], g4[0:2, :, None, :], g4[2:4, :, None, :],
                 mlstm_gate_b[0][0:2, :, None, None], mlstm_gate_b[0][2:4, :, None, None])

    w9 = mlstm_conv_w[0].reshape(9, 2 * D_C)
    cb = mlstm_conv_b[0].reshape(1, 2 * D_C)
    qk = _conv(proj1, w9, cb)

    tri_m = _tri_consts(L_MLSTM)
    mconsts = (jnp.asarray(tri_m, BF16), jnp.asarray(np.swapaxes(tri_m, 1, 2), BF16), jnp.asarray(tri_m))
    h2, cfin, nfin, mfin = _mlstm(qk, proj1, gate_args, mconsts, None, BATCH, SEQ, 0, want_final=True)
    state = (state_mlstm_C[:, 0], state_mlstm_n[:, 0][:, :, :, None, :], state_mlstm_m[:, 0][:, :, :, None, None])
    (h2,) = _mlstm(qk, proj1, gate_args, mconsts, h2, DEC_BATCH, DEC_SEQ, N_P, state=state)
    y1 = _mlstm_combine(h2, proj1, mlstm_norm_g[0].reshape(1, D_C))

    y = _out_proj(y1, y1, 0, 1, w_out_odd[0].astype(BF16), x1, mod_rows[1], final_norm_g.reshape(1, D_MODEL))

    y_prompt = y[:N_P].reshape(BATCH, SEQ, D_MODEL).astype(dt)
    y_sample = y[N_P:].reshape(DEC_BATCH, DEC_SEQ, D_MODEL).astype(dt)
    new_mlstm_C = cfin[:, None].astype(dt)
    new_mlstm_n = nfin[:, :, :, 0][:, None].astype(dt)
    new_mlstm_m = mfin[:, :, :, 0, 0][:, None].astype(dt)
    return (y_prompt, y_sample, new_hgrn, new_rwkv, new_mlstm_C, new_mlstm_n, new_mlstm_m)
```

```python
import functools

import numpy as np
import jax
import jax.numpy as jnp
from jax import lax
from jax.experimental import pallas as pl
from jax.experimental.pallas import tpu as pltpu

F32 = jnp.float32
BF16 = jnp.bfloat16

D_MODEL = 1024
BATCH, SEQ = 16, 256
DEC_BATCH, DEC_SEQ = 2, 2048
GRID_W = 64
N_P = BATCH * SEQ
N_S = DEC_BATCH * DEC_SEQ
N_TOK = N_P + N_S
D_A, DK_A, H_A = 1024, 128, 8
D_B, HS_B, H_B = 1024, 64, 16
R_LOW = 64
D_C, H_C, DH_C = 2048, 4, 512
SHIFT_W_B = 3 * D_B + 4 * R_LOW
IN_EVEN = 5 * D_A + SHIFT_W_B + D_B
IN_EVEN_PAD = 9728
IN_ODD_MAIN = 5 * D_C
EPS = 1e-6
GN_EPS_B = 64e-5

ROW_BLK = 256
SEQ_BLK = 2048
L_HGRN = 64
HGRN_HEADS = 2
L_RWKV = 32
SB_RWKV = 128
NSB_RWKV = 2
RWKV_VMEM_BYTES = 48 * 1024 * 1024
L_MLSTM = 256
LANE = 128

NN = (((1,), (0,)), ((), ()))
NT = (((1,), (1,)), ((), ()))
TN = (((0,), (0,)), ((), ()))


def _dot(a, b, dims=NN):
    return lax.dot_general(a.astype(BF16), b.astype(BF16), dims, preferred_element_type=F32)


def _split2(x):
    hi = x.astype(BF16)
    lo = (x - hi.astype(F32)).astype(BF16)
    return hi, lo


def _dot_sel(g, x):
    hi, lo = _split2(x)
    return (lax.dot_general(g, hi, NN, preferred_element_type=F32)
            + lax.dot_general(g, lo, NN, preferred_element_type=F32))


def _dot_sel_r(x, g):
    hi, lo = _split2(x)
    return (lax.dot_general(hi, g, NN, preferred_element_type=F32)
            + lax.dot_general(lo, g, NN, preferred_element_type=F32))


def _sigmoid(x):
    return 1.0 / (1.0 + jnp.exp(-x))


def _silu(x):
    return x * _sigmoid(x)


def _logsig(x):
    return jnp.minimum(x, 0.0) - jnp.log(1.0 + jnp.exp(-jnp.abs(x)))


def _hgrn_consts(L):
    nlev = int(np.log2(L))
    t = np.arange(L)
    G = np.zeros((nlev + 2, L, L), np.float32)
    G[0] = t[None, :] <= t[:, None]
    G[1] = t[None, :] > t[:, None]
    M = np.zeros((nlev + 1, L, L), np.float32)
    M[0] = np.eye(L)
    for lev in range(nlev):
        m = 1 << lev
        grp, pos = t // (2 * m), t % (2 * m)
        anchor = grp * 2 * m + m - 1
        for r in range(L):
            if pos[r] >= m:
                G[2 + lev, r, anchor[r] + 1:r + 1] = 1.0
            else:
                G[2 + lev, r, r + 1:anchor[r] + 1] = 1.0
        M[1 + lev] = (grp[:, None] == grp[None, :]) & (pos[:, None] >= m) & (pos[None, :] < m)
    G2 = np.stack([G.reshape(-1, L), G[:, ::-1, ::-1].reshape(-1, L)])
    M2 = np.stack([M, M[:, ::-1, ::-1]])
    return jnp.asarray(G2, BF16), jnp.asarray(M2, F32)


def _tri_consts(L):
    t = np.arange(L)
    return np.stack([t[None, :] <= t[:, None], t[None, :] >= t[:, None]]).astype(np.float32)


def _head_sum_const():
    i = np.arange(LANE)
    return jnp.asarray((i[:, None] // HS_B) == (i[None, :] // HS_B), BF16)


def _hgrn_chunks(probs, M_ref):
    L = probs[0][0].shape[0]
    nlev = M_ref.shape[1] - 1
    st = []
    for (q, v, pre, lb, St, G, d) in probs:
        f = lb + (1.0 - lb) * _sigmoid(pre)
        logf = jnp.log(f)
        st.append(dict(q=q, v=v, St=St, G=G, d=d, logf=logf, k=1.0 - f))
    for p in st:
        p["E"] = _dot_sel(p["G"], p["logf"])
    for p in st:
        p["o"] = _dot(p["q"] * jnp.exp(p["E"][0:L]), p["St"], NT)
    for p in st:
        p["att"] = M_ref[p["d"], 0] * _dot(p["q"], p["k"], NT)
    for lev in range(nlev):
        for p in st:
            F = jnp.exp(p["E"][(2 + lev) * L:(3 + lev) * L])
            p["att"] = p["att"] + M_ref[p["d"], 1 + lev] * _dot(p["q"] * F, p["k"] * F, NT)
    for p in st:
        p["o"] = p["o"] + _dot(p["att"], p["v"])
    out = []
    for p in st:
        tot = jnp.sum(p["logf"], axis=0, keepdims=True)
        St = jnp.exp(tot) * p["St"] + _dot(p["v"], p["k"] * jnp.exp(p["E"][L:2 * L]), TN)
        out.append((p["o"], St))
    return out


def _rwkv_consts(L, SB):
    t = np.arange(SB)
    same = (t[:, None] // L) == (t[None, :] // L)
    le = t[None, :] <= t[:, None]
    ge = t[None, :] >= t[:, None]
    lt = t[None, :] < t[:, None]
    gt = t[None, :] > t[:, None]
    sel = np.stack([np.concatenate([same & le, same & gt]), np.concatenate([same & ge, same & lt])])
    strict = np.stack([same & lt, same & gt])
    incl = np.stack([same & le, same & ge])
    return jnp.asarray(sel, BF16), jnp.asarray(strict, F32), jnp.asarray(incl, F32)


def _swap_heads(x):
    return pltpu.roll(x, HS_B, 1)


def _rwkv_block(L, r, lw, kd, v, vsw, kk, bv, sel, strict, incl, lm0, lm1):
    return _rwkv_blocks(L, [(r, lw, kd, v, vsw, kk, bv, sel, strict, incl)], lm0, lm1)[0]


def _rwkv_blocks(L, probs, lm0, lm1):
    SB = probs[0][0].shape[0]
    st = []
    for (r, lw, kd, v, vsw, kk, bv, sel, strict, incl) in probs:
        gg = _dot_sel(sel, lw)
        st.append(dict(r=r, lw=lw, kd=kd, v=v, vsw=vsw, kk=kk, bv=bv, strict=strict, incl=incl,
                       ginc=gg[:SB], gend=gg[SB:]))
    for p in st:
        e_inc = jnp.exp(p["ginc"])
        e_exc = jnp.exp(p["ginc"] - p["lw"])
        e_neg = jnp.exp(-p["ginc"])
        e_end = jnp.exp(p["gend"])
        p["Kq"], p["Rq"] = p["kk"] * e_exc, p["r"] * e_inc
        Bi, Ki = p["bv"] * e_neg, p["kd"] * e_neg
        p["Be"], p["Ke"] = p["bv"] * e_end, p["kd"] * e_end
        p["lhs"] = jnp.concatenate([p["Kq"] * lm0, p["Kq"] * lm1, p["Rq"] * lm0, p["Rq"] * lm1], axis=0)
        p["rhs"] = jnp.concatenate([Bi, Ki], axis=0)
    for p in st:
        p["sc"] = _dot(p["lhs"], p["rhs"], NT)
    ch = []
    for p in st:
        for h in (0, 1):
            sc = p["sc"]
            ch.append(dict(p=p, h=h,
                           Mb=p["strict"] * sc[h * SB:(h + 1) * SB, :SB],
                           Mk=p["strict"] * sc[h * SB:(h + 1) * SB, SB:],
                           Nb=p["incl"] * sc[(2 + h) * SB:(3 + h) * SB, :SB],
                           Nk=p["incl"] * sc[(2 + h) * SB:(3 + h) * SB, SB:]))
    for c in ch:
        c["MkV"] = _dot(c["Mk"], c["p"]["vsw"])
    for c in ch:
        lmh, lmo = (lm0, lm1) if c["h"] == 0 else (lm1, lm0)
        c["X"] = lmh * c["p"]["Kq"] + lmo * c["MkV"]
        c["Mp"] = c["Mb"]
    for c in ch:
        c["X"] = c["X"] - _dot(c["Mb"], c["X"])
    n = 2
    while n < L:
        for c in ch:
            c["Mp"] = _dot(c["Mp"], c["Mp"])
        for c in ch:
            c["X"] = c["X"] + _dot(c["Mp"], c["X"])
        n *= 2
    for c in ch:
        c["G"] = _dot(c["Nb"], c["X"])
    for c in ch:
        c["NkV"] = _dot(c["Nk"], c["p"]["v"])
    out = []
    for i, p in enumerate(st):
        c0, c1 = ch[2 * i], ch[2 * i + 1]
        A1 = lm0 * c0["X"] + lm1 * c1["X"]
        A2 = _swap_heads(lm0 * c1["X"] + lm1 * c0["X"])
        Qe = p["Rq"] - (lm0 * c0["G"] + lm1 * c1["G"])
        Yl = lm0 * c0["NkV"] + lm1 * c1["NkV"] - _swap_heads(lm0 * c1["G"] + lm1 * c0["G"])
        out.append((Qe, Yl, A1, A2, p["Be"], p["Ke"], p["ginc"] + p["gend"]))
    return out


def _mlstm_chunk(q, k, v, gi_col, gf_col, gi_row, gf_row, C, n, m, tri, tri_t, causal):
    L = q.shape[0]
    lf_col = _logsig(gf_col)
    lf_row = _logsig(gf_row)
    b_col = _dot_sel(tri, jnp.broadcast_to(lf_col, (L, LANE)))[:, 0:1]
    b_row = _dot_sel_r(jnp.broadcast_to(lf_row, (8, L)), tri_t)[0:1, :]
    btot = jnp.sum(lf_col, axis=0, keepdims=True)
    log_d = jnp.where(causal > 0, b_col - b_row + gi_row, -jnp.inf)
    m_loc = jnp.max(log_d, axis=1, keepdims=True)
    log_prev = b_col + m
    m_t = jnp.maximum(log_prev, m_loc)
    w_prev = jnp.exp(log_prev - m_t)
    s = _dot(q, k, NT) * jnp.exp(log_d - m_t)
    num = w_prev * _dot(q, C) + _dot(s, v)
    den = w_prev * jnp.sum(q * n, axis=1, keepdims=True) + jnp.sum(s, axis=1, keepdims=True)
    h = num / jnp.maximum(jnp.abs(den), jnp.exp(-m_t))
    log_s = btot - b_col + gi_col
    m_new = jnp.maximum(btot + m, jnp.max(log_s, axis=0, keepdims=True))
    kw = k * jnp.exp(log_s - m_new)
    w_old = jnp.exp(btot + m - m_new)
    C = w_old * C + _dot(kw, v, TN)
    n = w_old * n + jnp.sum(kw, axis=0, keepdims=True)
    return h, C, n, m_new


def _adaln_kernel(c_ref, w_ref, b_ref, o_ref):
    o_ref[0] = _dot(_silu(c_ref[...]), w_ref[0]) + b_ref[0]


def _adaln(cond8, w_mod, b_mod):
    depth, _, n3 = w_mod.shape
    tn = 1024
    return pl.pallas_call(
        _adaln_kernel,
        grid=(depth, n3 // tn),
        in_specs=[pl.BlockSpec((8, D_MODEL), lambda l, j: (0, 0)),
                  pl.BlockSpec((1, D_MODEL, tn), lambda l, j: (l, 0, j)),
                  pl.BlockSpec((1, 1, tn), lambda l, j: (l, 0, j))],
        out_specs=pl.BlockSpec((1, 8, tn), lambda l, j: (l, 0, j)),
        out_shape=jax.ShapeDtypeStruct((depth, 8, n3), F32),
        name="adaln",
    )(cond8, w_mod, b_mod.reshape(depth, 1, n3))


def _rms_mod(x, g, mod):
    y = x * lax.rsqrt(jnp.mean(x * x, axis=-1, keepdims=True) + EPS) * g
    return y * (1.0 + mod[:, D_MODEL:2 * D_MODEL]) + mod[:, 0:D_MODEL]


def _norm_mod_kernel(x_ref, g_ref, mod_ref, h_ref):
    h_ref[...] = _rms_mod(x_ref[...], g_ref[...], mod_ref[0]).astype(BF16)


def _norm_mod(x, g, mod_rows):
    nblk = x.shape[0] // ROW_BLK
    return pl.pallas_call(
        _norm_mod_kernel,
        grid=(nblk,),
        in_specs=[pl.BlockSpec((ROW_BLK, D_MODEL), lambda i: (i, 0)),
                  pl.BlockSpec((1, D_MODEL), lambda i: (0, 0)),
                  pl.BlockSpec((1, 1, 3 * D_MODEL), lambda i: (i, 0, 0))],
        out_specs=pl.BlockSpec((ROW_BLK, D_MODEL), lambda i: (i, 0)),
        out_shape=jax.ShapeDtypeStruct(x.shape, BF16),
        name="norm_mod",
    )(x, g, mod_rows)


def _matmul_kernel(a_ref, w_ref, o_ref):
    o_ref[...] = jnp.dot(a_ref[...], w_ref[...], preferred_element_type=F32)


def _matmul(a, w, tm=1024, tn=512):
    m, k = a.shape
    n = w.shape[1]
    tn = min(tn, n)
    tiles = 2 * (tm * k * 2 + k * tn * 2 + tm * tn * 4)
    return pl.pallas_call(
        _matmul_kernel,
        grid=(m // tm, n // tn),
        in_specs=[pl.BlockSpec((tm, k), lambda i, j: (i, 0)),
                  pl.BlockSpec((k, tn), lambda i, j: (0, j))],
        out_specs=pl.BlockSpec((tm, tn), lambda i, j: (i, j)),
        out_shape=jax.ShapeDtypeStruct((m, n), F32),
        compiler_params=pltpu.CompilerParams(vmem_limit_bytes=tiles + tiles // 4),
        name="in_proj",
    )(a, w)


def _add_prev(in_specs, args, prev):
    if prev is None:
        return {}
    in_specs.append(pl.BlockSpec(memory_space=pl.ANY))
    args.append(prev)
    return {len(args) - 1: 0}


def _hgrn_kernel(T, has_state, has_prev, want_final, *refs):
    q_ref, v_ref, ff_ref, fb_ref, z_ref, lb_ref, gn_ref, G_ref, M_ref = refs[:9]
    pos = 9
    s0_ref = None
    if has_state:
        s0_ref = refs[pos]
        pos += 1
    pos += int(has_prev)
    out_ref = refs[pos]
    pos += 1
    sfin_ref = None
    if want_final:
        sfin_ref = refs[pos]
        pos += 1
    o_dir = refs[pos]
    L = L_HGRN
    nc = T // L
    f_refs = (ff_ref, fb_ref)
    chains = [(hh, d) for hh in range(HGRN_HEADS) for d in (0, 1)]
    lanes = lambda hh: slice(hh * DK_A, (hh + 1) * DK_A)

    def step(c, carry):
        probs, where = [], []
        for i, (hh, d) in enumerate(chains):
            ci = c if d == 0 else nc - 1 - c
            rows = pl.ds(pl.multiple_of(ci * L, L), L)
            probs.append((q_ref[rows, lanes(hh)], v_ref[rows, lanes(hh)], f_refs[d][rows, lanes(hh)],
                          lb_ref[hh], carry[i], G_ref[d], d))
            where.append(rows)
        res = _hgrn_chunks(probs, M_ref)
        for (hh, d), rows, (o, _) in zip(chains, where, res):
            o_dir[d, rows, lanes(hh)] = o
        return tuple(st for _, st in res)

    if has_state:
        init = tuple(s0_ref[0, d, hh] for hh, d in chains)
    else:
        init = tuple(jnp.zeros((DK_A, DK_A), F32) for _ in chains)
    fin = lax.fori_loop(0, nc, step, init)
    if want_final:
        for (hh, d), st in zip(chains, fin):
            sfin_ref[0, d, hh] = st
    for hh in range(HGRN_HEADS):
        o = o_dir[0, :, lanes(hh)] + o_dir[1, :, lanes(hh)]
        y = o * lax.rsqrt(jnp.mean(o * o, axis=-1, keepdims=True) + EPS) * gn_ref[hh]
        out_ref[:, lanes(hh)] = (y * _silu(z_ref[:, lanes(hh)])).astype(BF16)


def _hgrn(proj, lb, gn, consts, prev_out, nb, T, row0, s0_t=None, want_final=False):
    G, M = consts
    rb0 = row0 // T
    nh = HGRN_HEADS
    w = nh * DK_A
    col = lambda off: pl.BlockSpec((T, w), lambda b, h, off=off: (rb0 + b, off // nh + h))
    in_specs = [col(0), col(8), col(16), col(24), col(32),
                pl.BlockSpec((nh, 1, LANE), lambda b, h: (h, 0, 0)),
                pl.BlockSpec((nh, 1, LANE), lambda b, h: (h, 0, 0)),
                pl.BlockSpec(G.shape, lambda b, h: (0, 0, 0)),
                pl.BlockSpec(M.shape, lambda b, h: (0, 0, 0, 0))]
    args = [proj, proj, proj, proj, proj, lb, gn, G, M]
    if s0_t is not None:
        in_specs.append(pl.BlockSpec((1, 2, nh, DK_A, DK_A), lambda b, h: (b, 0, h, 0, 0)))
        args.append(s0_t)
    aliases = _add_prev(in_specs, args, prev_out)
    out_specs = [pl.BlockSpec((T, w), lambda b, h: (rb0 + b, h))]
    out_shape = [jax.ShapeDtypeStruct((N_TOK, D_A), BF16)]
    if want_final:
        out_specs.append(pl.BlockSpec((1, 2, nh, DK_A, DK_A), lambda b, h: (b, 0, h, 0, 0)))
        out_shape.append(jax.ShapeDtypeStruct((nb, 2, H_A, DK_A, DK_A), F32))
    res = pl.pallas_call(
        functools.partial(_hgrn_kernel, T, s0_t is not None, prev_out is not None, want_final),
        grid=(nb, H_A // nh),
        in_specs=in_specs,
        out_specs=out_specs,
        out_shape=out_shape,
        scratch_shapes=[pltpu.VMEM((2, T, w), F32)],
        input_output_aliases=aliases,
        name="hgrn2_T%d" % T,
    )(*args)
    return res


def _seq_pos(shape):
    is_latent = pl.program_id(0) >= N_P // SEQ_BLK
    seq_len = jnp.where(is_latent, DEC_SEQ, SEQ)
    t = lax.broadcasted_iota(jnp.int32, shape, 0)
    return jnp.bitwise_and(t, seq_len - 1), seq_len, is_latent


def _shift_kernel(p_ref, mu_ref, o_ref):
    p = p_ref[...]
    pos, seq_len, _ = _seq_pos(p.shape)
    prev = jnp.where(pos == 0, 0.0, pltpu.roll(p, 1, 0))
    nxt = jnp.where(pos == seq_len - 1, 0.0, pltpu.roll(p, p.shape[0] - 1, 0))
    mu = mu_ref[...]
    o_ref[...] = p + mu[0:1] * (prev - p) + mu[1:2] * (nxt - p)


def _shift(proj, mu):
    tc = 256
    c0 = (5 * D_A) // tc
    return pl.pallas_call(
        _shift_kernel,
        grid=(N_TOK // SEQ_BLK, SHIFT_W_B // tc),
        in_specs=[pl.BlockSpec((SEQ_BLK, tc), lambda i, j: (i, c0 + j)),
                  pl.BlockSpec((2, tc), lambda i, j: (0, j))],
        out_specs=pl.BlockSpec((SEQ_BLK, tc), lambda i, j: (i, j)),
        out_shape=jax.ShapeDtypeStruct((N_TOK, SHIFT_W_B), F32),
        name="rwkv_shift",
    )(proj, mu)


def _rwkv_prep_kernel(r_ref, k_ref, v_ref, low_ref, w0_ref, w2_ref, a0_ref, a2_ref, kk_ref_p, ka_ref, rk_ref,
                      bd_ref, kk_ref, lw_ref, kd_ref, bv_ref, bonus_ref):
    r, k, v, low = r_ref[...], k_ref[...], v_ref[...], low_ref[...]
    bd = bd_ref[...]
    tl = jnp.tanh(low)
    kkr = k * kk_ref_p[...]
    kk = kkr / jnp.maximum(jnp.sqrt(_dot_sel_r(kkr * kkr, bd)), 1e-12)
    kk_ref[...] = kk
    bonus = jnp.zeros_like(r)
    for d in (0, 1):
        u = w0_ref[d] + _dot(tl, w2_ref[d])
        lw_ref[d] = -np.float32(np.exp(-0.5)) * _sigmoid(u)
        a = _sigmoid(a0_ref[d] + _dot(low, a2_ref[d]))
        kd = k * (1.0 + (a - 1.0) * ka_ref[...])
        kd_ref[d] = kd
        bv_ref[d] = a * kk
        bonus = bonus + _dot_sel_r(r * kd * rk_ref[...], bd) * v
    bonus_ref[...] = bonus


def _rwkv_prep(sh, w0, w2p, a0, a2p, k_k, k_a, r_k, bd, tm=1024):
    n = sh.shape[0]
    tile = lambda off: pl.BlockSpec((tm, LANE), lambda i, j, off=off: (i, off + j))
    vec = pl.BlockSpec((1, LANE), lambda i, j: (0, j))
    dvec = pl.BlockSpec((2, 1, LANE), lambda i, j: (0, 0, j))
    dmat = pl.BlockSpec((2, 4 * R_LOW, LANE), lambda i, j: (0, 0, j))
    dout = pl.BlockSpec((2, tm, LANE), lambda i, j: (0, i, j))
    one = jax.ShapeDtypeStruct((n, D_B), F32)
    two = jax.ShapeDtypeStruct((2, n, D_B), F32)
    return pl.pallas_call(
        _rwkv_prep_kernel,
        grid=(n // tm, D_B // LANE),
        in_specs=[tile(0), tile(8), tile(16),
                  pl.BlockSpec((tm, 4 * R_LOW), lambda i, j: (i, (3 * D_B) // (4 * R_LOW))),
                  dvec, dmat, dvec, dmat, vec, vec, vec,
                  pl.BlockSpec((LANE, LANE), lambda i, j: (0, 0))],
        out_specs=[tile(0), dout, dout, dout, tile(0)],
        out_shape=[one, two, two, two, one],
        name="rwkv_prep",
    )(sh, sh, sh, sh, w0, w2p, a0, a2p, k_k, k_a, r_k, bd)


def _rwkv_kernel(T, has_state, has_prev, want_final, *refs):
    (r_ref, v_ref, kk_ref, lw_ref, kd_ref, bv_ref, sel_ref, strict_ref, incl_ref, bdeye_ref) = refs[:10]
    pos = 10
    s0_ref = None
    if has_state:
        s0_ref = refs[pos]
        pos += 1
    pos += int(has_prev)
    y_ref = refs[pos]
    pos += 1
    sfin_ref = None
    if want_final:
        sfin_ref = refs[pos]
        pos += 1
    tm_scr, u_scr, qe_scr, s_scr = refs[pos:pos + 4]
    L, SB = L_RWKV, SB_RWKV
    nc = T // L
    cps = SB // L

    def block_step(it, _):
        lane = lax.broadcasted_iota(jnp.int32, (1, LANE), 1)
        lm0 = (lane < HS_B).astype(F32)
        lm1 = 1.0 - lm0
        probs, keys, vs = [], [], {}
        for s in range(NSB_RWKV):
            sb = it * NSB_RWKV + s
            rows = pl.ds(pl.multiple_of(sb * SB, SB), SB)
            r, v, kk = r_ref[rows, :], v_ref[rows, :], kk_ref[rows, :]
            vsw = _swap_heads(v)
            vs[s] = v
            for d in (0, 1):
                probs.append((r, lw_ref[d, rows, :], kd_ref[d, rows, :], v, vsw, kk, bv_ref[d, rows, :],
                              sel_ref[d], strict_ref[d], incl_ref[d]))
                keys.append((s, sb, rows, d))
        res = _rwkv_blocks(L, probs, lm0, lm1)
        for (s, sb, rows, d), out in zip(keys, res):
            y_ref[d, rows, :] = out[1]
            qe_scr[d, rows, :] = out[0].astype(BF16)
        for j in range(cps):
            rc = slice(j * L, (j + 1) * L)
            for (s, sb, rows, d), (_, _, A1, A2, Be, Ke, gt) in zip(keys, res):
                decay = bdeye_ref[1] * jnp.exp(gt[j * L:j * L + 1])
                tm_scr[d, sb * cps + j] = (decay - bdeye_ref[0] * _dot(A1[rc], Be[rc], TN)).astype(BF16)
            for (s, sb, rows, d), (_, _, A1, A2, Be, Ke, gt) in zip(keys, res):
                u_scr[d, sb * cps + j] = bdeye_ref[0] * _dot(
                    jnp.concatenate([vs[s][rc], -A2[rc]], axis=0), jnp.concatenate([Ke[rc], Be[rc]], axis=0), TN)
        return 0

    lax.fori_loop(0, T // (SB * NSB_RWKV), block_step, 0)

    def chunk_step(c, _):
        for d in (0, 1):
            ci = c if d == 0 else nc - 1 - c
            rows = pl.ds(pl.multiple_of(ci * L, L), L)
            S = s_scr[d].astype(BF16)
            y_ref[d, rows, :] = y_ref[d, rows, :] + lax.dot_general(qe_scr[d, rows, :], S, NT,
                                                                    preferred_element_type=F32)
            s_scr[d] = lax.dot_general(S, tm_scr[d, ci], NN, preferred_element_type=F32) + u_scr[d, ci]
        return 0

    zero = jnp.zeros((HS_B, HS_B), F32)
    for d in (0, 1):
        if has_state:
            top = jnp.concatenate([s0_ref[0, d, 0], zero], axis=1)
            bot = jnp.concatenate([zero, s0_ref[0, d, 1]], axis=1)
            s_scr[d] = jnp.concatenate([top, bot], axis=0)
        else:
            s_scr[d] = jnp.zeros((LANE, LANE), F32)
    lax.fori_loop(0, nc, chunk_step, 0)
    if want_final:
        for d in (0, 1):
            fin = s_scr[d]
            sfin_ref[0, d, 0] = fin[0:HS_B, 0:HS_B]
            sfin_ref[0, d, 1] = fin[HS_B:LANE, HS_B:LANE]


def _rwkv(sh, kk, lw, kd, bv, consts, prev_y, nb, T, row0, s0=None, want_final=False):
    sel, strict, incl, bdeye = consts
    rb0 = row0 // T
    nhp = H_B // 2
    nc = T // L_RWKV
    tok = lambda off: pl.BlockSpec((T, LANE), lambda b, p, off=off: (rb0 + b, off + p))
    dtok = pl.BlockSpec((2, T, LANE), lambda b, p: (0, rb0 + b, p))
    cst = lambda a: pl.BlockSpec(a.shape, lambda b, p: (0, 0, 0))
    st = pl.BlockSpec((1, 2, 2, HS_B, HS_B), lambda b, p: (b, 0, p, 0, 0))
    in_specs = [tok(0), tok(2 * D_B // LANE), tok(0), dtok, dtok, dtok, cst(sel), cst(strict), cst(incl), cst(bdeye)]
    args = [sh, sh, kk, lw, kd, bv, sel, strict, incl, bdeye]
    if s0 is not None:
        in_specs.append(st)
        args.append(s0)
    aliases = _add_prev(in_specs, args, prev_y)
    out_specs = [dtok]
    out_shape = [jax.ShapeDtypeStruct((2, N_TOK, D_B), F32)]
    if want_final:
        out_specs.append(st)
        out_shape.append(jax.ShapeDtypeStruct((nb, 2, H_B, HS_B, HS_B), F32))
    return pl.pallas_call(
        functools.partial(_rwkv_kernel, T, s0 is not None, prev_y is not None, want_final),
        grid=(nb, nhp),
        in_specs=in_specs,
        out_specs=out_specs,
        out_shape=out_shape,
        scratch_shapes=[pltpu.VMEM((2, nc, LANE, LANE), BF16), pltpu.VMEM((2, nc, LANE, LANE), F32),
                        pltpu.VMEM((2, T, LANE), BF16), pltpu.VMEM((2, LANE, LANE), F32)],
        input_output_aliases=aliases,
        compiler_params=pltpu.CompilerParams(vmem_limit_bytes=RWKV_VMEM_BYTES),
        name="rwkv7_T%d" % T,
    )(*args)


def _rwkv_combine_kernel(y_ref, bonus_ref, z_ref, g_ref, b_ref, bd_ref, o_ref):
    bd = bd_ref[...]
    y = y_ref[0] + y_ref[1]
    yc = y - _dot_sel_r(y, bd) * (1.0 / HS_B)
    var = _dot_sel_r(yc * yc, bd) * (1.0 / HS_B)
    yn = yc * lax.rsqrt(var + GN_EPS_B) * g_ref[...] + b_ref[...]
    o_ref[...] = ((yn + bonus_ref[...]) * _silu(z_ref[...])).astype(BF16)


def _rwkv_combine(y2, bonus, proj, gn_g, gn_b, bd, tm=1024):
    n = bonus.shape[0]
    zc0 = (5 * D_A + SHIFT_W_B) // LANE
    tile = pl.BlockSpec((tm, LANE), lambda i, j: (i, j))
    vec = pl.BlockSpec((1, LANE), lambda i, j: (0, j))
    return pl.pallas_call(
        _rwkv_combine_kernel,
        grid=(n // tm, D_B // LANE),
        in_specs=[pl.BlockSpec((2, tm, LANE), lambda i, j: (0, i, j)), tile,
                  pl.BlockSpec((tm, LANE), lambda i, j: (i, zc0 + j)), vec, vec,
                  pl.BlockSpec((LANE, LANE), lambda i, j: (0, 0))],
        out_specs=tile,
        out_shape=jax.ShapeDtypeStruct((n, D_B), BF16),
        name="rwkv_combine",
    )(y2, bonus, proj, gn_g, gn_b, bd)


def _out_proj_kernel(final, a_ref, b_ref, w_ref, x_ref, mod_ref, g_ref, *rest):
    acc = jnp.dot(a_ref[...], w_ref[0:D_MODEL, :], preferred_element_type=F32)
    acc = acc + jnp.dot(b_ref[...], w_ref[D_MODEL:2 * D_MODEL, :], preferred_element_type=F32)
    x = x_ref[...] + mod_ref[0][:, 2 * D_MODEL:3 * D_MODEL] * acc
    if final:
        (y_ref,) = rest
        y_ref[...] = x * lax.rsqrt(jnp.mean(x * x, axis=-1, keepdims=True) + EPS) * g_ref[...]
    else:
        nmod_ref, x_out_ref, h_ref = rest
        x_out_ref[...] = x
        h_ref[...] = _rms_mod(x, g_ref[...], nmod_ref[0]).astype(BF16)


def _out_proj(a, b, a_col, b_col, w, x, mod_rows, g, next_mod_rows=None):
    n = x.shape[0]
    nblk = n // ROW_BLK
    final = next_mod_rows is None
    row = lambda c: pl.BlockSpec((ROW_BLK, D_MODEL), lambda i, c=c: (i, c))
    modspec = pl.BlockSpec((1, 1, 3 * D_MODEL), lambda i: (i, 0, 0))
    in_specs = [row(a_col), row(b_col), pl.BlockSpec(w.shape, lambda i: (0, 0)), row(0), modspec,
                pl.BlockSpec((1, D_MODEL), lambda i: (0, 0))]
    args = [a, b, w, x, mod_rows, g]
    if final:
        out_specs = row(0)
        out_shape = jax.ShapeDtypeStruct((n, D_MODEL), F32)
    else:
        in_specs.append(modspec)
        args.append(next_mod_rows)
        out_specs = [row(0), row(0)]
        out_shape = [jax.ShapeDtypeStruct((n, D_MODEL), F32), jax.ShapeDtypeStruct((n, D_MODEL), BF16)]
    return pl.pallas_call(
        functools.partial(_out_proj_kernel, final),
        grid=(nblk,),
        in_specs=in_specs,
        out_specs=out_specs,
        out_shape=out_shape,
        name="out_proj_final" if final else "out_proj",
    )(*args)


def _conv_taps(x_ref, w_ref, b_ref, o_ref, two_d, seq_len):
    n, tc = x_ref.shape
    t = lax.broadcasted_iota(jnp.int32, (n, LANE), 0)
    pos = jnp.bitwise_and(t, seq_len - 1)
    line = GRID_W if two_d else seq_len
    col = jnp.bitwise_and(pos, line - 1)
    not_last = (col != line - 1).astype(F32)
    not_first = (col != 0).astype(F32)
    if two_d:
        r = jnp.right_shift(pos, int(np.log2(GRID_W)))
        has_up = (r != 0).astype(F32)
        has_down = (r != seq_len // GRID_W - 1).astype(F32)
    for j in range(tc // LANE):
        cs = slice(j * LANE, (j + 1) * LANE)
        x = x_ref[:, cs]
        left = pltpu.roll(x * not_last, 1, 0)
        right = pltpu.roll(x * not_first, n - 1, 0)
        wrow = lambda i: (w_ref[3 * i:3 * i + 1, cs] * left + w_ref[3 * i + 1:3 * i + 2, cs] * x
                          + w_ref[3 * i + 2:3 * i + 3, cs] * right)
        acc = wrow(1) + b_ref[:, cs]
        if two_d:
            acc = acc + has_up * pltpu.roll(wrow(0), GRID_W, 0) + has_down * pltpu.roll(wrow(2), n - GRID_W, 0)
        o_ref[:, cs] = _silu(acc)


def _conv_kernel(x_ref, w_ref, b_ref, o_ref):
    is_latent = pl.program_id(0) >= N_P // SEQ_BLK

    @pl.when(jnp.logical_not(is_latent))
    def _():
        _conv_taps(x_ref, w_ref, b_ref, o_ref, False, SEQ)

    @pl.when(is_latent)
    def _():
        _conv_taps(x_ref, w_ref, b_ref, o_ref, True, DEC_SEQ)


def _conv(proj, w9, bias, tc=256):
    return pl.pallas_call(
        _conv_kernel,
        grid=(N_TOK // SEQ_BLK, (2 * D_C) // tc),
        in_specs=[pl.BlockSpec((SEQ_BLK, tc), lambda i, j: (i, j)),
                  pl.BlockSpec((9, tc), lambda i, j: (0, j)),
                  pl.BlockSpec((1, tc), lambda i, j: (0, j))],
        out_specs=pl.BlockSpec((SEQ_BLK, tc), lambda i, j: (i, j)),
        out_shape=jax.ShapeDtypeStruct((N_TOK, 2 * D_C), F32),
        name="mlstm_conv",
    )(proj, w9, bias)


def _mlstm_kernel(nc, has_state, has_prev, want_final, *refs):
    (q_ref, k_ref, v_ref, ic_ref, fc_ref, ir_ref, fr_ref, bi_ref, bf_ref,
     tri_ref, trit_ref, causal_ref) = refs[:12]
    pos = 12
    if has_state:
        c0_ref, n0_ref, m0_ref = refs[pos:pos + 3]
        pos += 3
    pos += int(has_prev)
    h_ref = refs[pos]
    pos += 1
    if want_final:
        cf_ref, nf_ref, mf_ref = refs[pos:pos + 3]
        pos += 3
    c_scr, n_scr, m_scr = refs[pos:pos + 3]
    c = pl.program_id(3)

    @pl.when(c == 0)
    def _():
        if has_state:
            c_scr[...] = c0_ref[0, 0, 0]
            n_scr[...] = n0_ref[0, 0, 0]
            m_scr[...] = m0_ref[0, 0, 0]
        else:
            c_scr[...] = jnp.zeros_like(c_scr)
            n_scr[...] = jnp.zeros_like(n_scr)
            m_scr[...] = jnp.zeros_like(m_scr)

    bi, bf = bi_ref[0, 0], bf_ref[0, 0]
    k = k_ref[...] * np.float32(DH_C ** -0.5)
    h, C, n, m = _mlstm_chunk(q_ref[...], k, v_ref[...],
                              ic_ref[0, 0] + bi, fc_ref[0, 0] + bf, ir_ref[0, 0] + bi, fr_ref[0, 0] + bf,
                              c_scr[...], n_scr[...], m_scr[...],
                              tri_ref[0], trit_ref[0], causal_ref[0])
    h_ref[0] = h
    c_scr[...] = C
    n_scr[...] = n
    m_scr[...] = m
    if want_final:
        @pl.when(c == nc - 1)
        def _():
            cf_ref[0, 0, 0] = C
            nf_ref[0, 0, 0] = n
            mf_ref[0, 0, 0] = m


def _mlstm(qk, proj, gates, consts, prev_h, nb, T, row0, state=None, want_final=False):
    ic, fc, ir, fr, bi, bf = gates
    tri, trit, causal = consts
    L = L_MLSTM
    nc = T // L
    rb0 = row0 // L
    blk = lambda b, d, c: rb0 + b * nc + c + d * (nc - 1 - 2 * c)
    tok = lambda off: pl.BlockSpec((L, DH_C), lambda b, h, d, c, off=off: (blk(b, d, c), off + h))
    gcol = pl.BlockSpec((1, 1, L, 1), lambda b, h, d, c: (d, h, blk(b, d, c), 0))
    grow = pl.BlockSpec((1, 1, 1, L), lambda b, h, d, c: (d, h, 0, blk(b, d, c)))
    gb = pl.BlockSpec((1, 1, 1, 1), lambda b, h, d, c: (d, h, 0, 0))
    cst = pl.BlockSpec((1, L, L), lambda b, h, d, c: (d, 0, 0))
    cs = pl.BlockSpec((1, 1, 1, DH_C, DH_C), lambda b, h, d, c: (b, d, h, 0, 0))
    ns = pl.BlockSpec((1, 1, 1, 1, DH_C), lambda b, h, d, c: (b, d, h, 0, 0))
    ms = pl.BlockSpec((1, 1, 1, 1, 1), lambda b, h, d, c: (b, d, h, 0, 0))
    in_specs = [tok(0), tok(H_C), tok(2 * H_C), gcol, gcol, grow, grow, gb, gb, cst, cst, cst]
    args = [qk, qk, proj, ic, fc, ir, fr, bi, bf, tri, trit, causal]
    if state is not None:
        in_specs += [cs, ns, ms]
        args += list(state)
    aliases = _add_prev(in_specs, args, prev_h)
    out_specs = [pl.BlockSpec((1, L, DH_C), lambda b, h, d, c: (d, blk(b, d, c), h))]
    out_shape = [jax.ShapeDtypeStruct((2, N_TOK, D_C), F32)]
    if want_final:
        out_specs += [cs, ns, ms]
        out_shape += [jax.ShapeDtypeStruct((nb, 2, H_C, DH_C, DH_C), F32),
                      jax.ShapeDtypeStruct((nb, 2, H_C, 1, DH_C), F32),
                      jax.ShapeDtypeStruct((nb, 2, H_C, 1, 1), F32)]
    return pl.pallas_call(
        functools.partial(_mlstm_kernel, nc, state is not None, prev_h is not None, want_final),
        grid=(nb, H_C, 2, nc),
        in_specs=in_specs,
        out_specs=out_specs,
        out_shape=out_shape,
        scratch_shapes=[pltpu.VMEM((DH_C, DH_C), F32), pltpu.VMEM((1, DH_C), F32), pltpu.VMEM((1, 1), F32)],
        input_output_aliases=aliases,
        name="mlstm_T%d" % T,
    )(*args)


def _mlstm_combine_kernel(h_ref, o_ref, z_ref, g_ref, y_ref):
    y = _sigmoid(o_ref[...]) * (h_ref[0] + h_ref[1])
    y = y * lax.rsqrt(jnp.mean(y * y, axis=-1, keepdims=True) + EPS) * g_ref[...]
    y_ref[...] = (y * _silu(z_ref[...])).astype(BF16)


def _mlstm_combine(h2, proj, norm_g, tm=512):
    n = proj.shape[0]
    return pl.pallas_call(
        _mlstm_combine_kernel,
        grid=(n // tm, H_C),
        in_specs=[pl.BlockSpec((2, tm, DH_C), lambda i, h: (0, i, h)),
                  pl.BlockSpec((tm, DH_C), lambda i, h: (i, 3 * H_C + h)),
                  pl.BlockSpec((tm, DH_C), lambda i, h: (i, 4 * H_C + h)),
                  pl.BlockSpec((1, DH_C), lambda i, h: (0, h))],
        out_specs=pl.BlockSpec((tm, DH_C), lambda i, h: (i, h)),
        out_shape=jax.ShapeDtypeStruct((n, D_C), BF16),
        name="mlstm_combine",
    )(h2, proj, proj, norm_g)


def kernel(x_prompt, x_sample, c, state_hgrn, state_rwkv, state_mlstm_C, state_mlstm_n, state_mlstm_m, c_ctx, w_mod, b_mod, norm_g, final_norm_g, w_in_even, w_out_even, hgrn_lb_logits, hgrn_norm_g, rwkv_shift_mu, rwkv_w0, rwkv_w2, rwkv_a0, rwkv_a2, rwkv_k_k, rwkv_k_a, rwkv_r_k, rwkv_gn_g, rwkv_gn_b, w_in_odd, w_out_odd, mlstm_conv_w, mlstm_conv_b, mlstm_gate_b, mlstm_norm_g):
    dt = x_prompt.dtype
    x = jnp.concatenate([x_prompt.reshape(N_P, D_MODEL), x_sample.reshape(N_S, D_MODEL)], axis=0)

    cond8 = jnp.zeros((8, D_MODEL), F32).at[0].set(c_ctx).at[1:1 + DEC_BATCH].set(c)
    mods = _adaln(cond8, w_mod, b_mod)
    blk_cond = np.concatenate([np.zeros(N_P // ROW_BLK, np.int32),
                               1 + np.repeat(np.arange(DEC_BATCH, dtype=np.int32), DEC_SEQ // ROW_BLK)])
    mod_rows = [mods[l][blk_cond][:, None, :] for l in range(2)]

    h0 = _norm_mod(x, norm_g[0:1], mod_rows[0])
    w_in0 = jnp.pad(w_in_even[0].astype(BF16), ((0, 0), (0, IN_EVEN_PAD - IN_EVEN)))
    proj = _matmul(h0, w_in0, tn=IN_EVEN_PAD // 4)

    lb_all = jnp.cumsum(jax.nn.softmax(hgrn_lb_logits.astype(F32), axis=0), axis=0)
    lb = lb_all[0].reshape(H_A, 1, DK_A)
    gn_a = hgrn_norm_g[0].reshape(H_A, 1, DK_A)
    hconsts = _hgrn_consts(L_HGRN)
    s_hgrn_t = jnp.swapaxes(state_hgrn[:, 0], -1, -2)
    out_a, sfin_a = _hgrn(proj, lb, gn_a, hconsts, None, BATCH, SEQ, 0, want_final=True)
    (out_a,) = _hgrn(proj, lb, gn_a, hconsts, out_a, DEC_BATCH, DEC_SEQ, N_P, s0_t=s_hgrn_t)
    new_hgrn = jnp.swapaxes(sfin_a, -1, -2)[:, None].astype(dt)

    mu = rwkv_shift_mu[0]
    sh = _shift(proj, mu)
    low_pad = lambda w, off: jnp.zeros((2, 4 * R_LOW, D_B), F32).at[0, off:off + R_LOW].set(w[0]).at[
        1, off + R_LOW:off + 2 * R_LOW].set(w[1]).astype(BF16)
    bd = _head_sum_const()
    kk, lw, kd, bv, bonus = _rwkv_prep(
        sh, rwkv_w0[0].reshape(2, 1, D_B), low_pad(rwkv_w2[0], 0), rwkv_a0[0].reshape(2, 1, D_B),
        low_pad(rwkv_a2[0], 2 * R_LOW), rwkv_k_k[0].reshape(1, D_B), rwkv_k_a[0].reshape(1, D_B),
        rwkv_r_k[0].reshape(1, D_B), bd)
    lane_head = np.arange(LANE) // HS_B
    bdeye = jnp.asarray(np.stack([lane_head[:, None] == lane_head[None, :], np.eye(LANE, dtype=bool)]), F32)
    rconsts = _rwkv_consts(L_RWKV, SB_RWKV) + (bdeye,)
    y2, sfin_b = _rwkv(sh, kk, lw, kd, bv, rconsts, None, BATCH, SEQ, 0, want_final=True)
    (y2,) = _rwkv(sh, kk, lw, kd, bv, rconsts, y2, DEC_BATCH, DEC_SEQ, N_P, s0=state_rwkv[:, 0])
    new_rwkv = sfin_b[:, None].astype(dt)
    out_b = _rwkv_combine(y2, bonus, proj, rwkv_gn_g[0].reshape(1, D_B), rwkv_gn_b[0].reshape(1, D_B), bd)

    x1, h1 = _out_proj(out_a, out_b, 0, 0, w_out_even[0].astype(BF16), x, mod_rows[0], norm_g[1:2],
                       next_mod_rows=mod_rows[1])

    w_in1 = w_in_odd[0].astype(BF16)
    proj1 = _matmul(h1, w_in1[:, :IN_ODD_MAIN], tn=IN_ODD_MAIN // 4)
    w_gate = jnp.pad(w_in1[:, IN_ODD_MAIN:], ((0, 0), (0, LANE - 4 * H_C)))
    gates = _matmul(h1, w_gate)[:, :4 * H_C]
    g4 = gates.reshape(N_TOK, 4, H_C).transpose(1, 2, 0)
    gate_args = (g4[0:2, :, :, None], g4[2:4, :, :, None], g4[0:2, :, None, :], g4[2:4, :, None, :],
                 mlstm_gate_b[0][0:2, :, None, None], mlstm_gate_b[0][2:4, :, None, None])

    w9 = mlstm_conv_w[0].reshape(9, 2 * D_C)
    cb = mlstm_conv_b[0].reshape(1, 2 * D_C)
    qk = _conv(proj1, w9, cb)

    tri_m = _tri_consts(L_MLSTM)
    mconsts = (jnp.asarray(tri_m, BF16), jnp.asarray(np.swapaxes(tri_m, 1, 2), BF16), jnp.asarray(tri_m))
    h2, cfin, nfin, mfin = _mlstm(qk, proj1, gate_args, mconsts, None, BATCH, SEQ, 0, want_final=True)
    state = (state_mlstm_C[:, 0], state_mlstm_n[:, 0][:, :, :, None, :], state_mlstm_m[:, 0][:, :, :, None, None])
    (h2,) = _mlstm(qk, proj1, gate_args, mconsts, h2, DEC_BATCH, DEC_SEQ, N_P, state=state)
    y1 = _mlstm_combine(h2, proj1, mlstm_norm_g[0].reshape(1, D_C))

    y = _out_proj(y1, y1, 0, 1, w_out_odd[0].astype(BF16), x1, mod_rows[1], final_norm_g.reshape(1, D_MODEL))

    y_prompt = y[:N_P].reshape(BATCH, SEQ, D_MODEL).astype(dt)
    y_sample = y[N_P:].reshape(DEC_BATCH, DEC_SEQ, D_MODEL).astype(dt)
    new_mlstm_C = cfin[:, None].astype(dt)
    new_mlstm_n = nfin[:, :, :, 0][:, None].astype(dt)
    new_mlstm_m = mfin[:, :, :, 0, 0][:, None].astype(dt)
    return (y_prompt, y_sample, new_hgrn, new_rwkv, new_mlstm_C, new_mlstm_n, new_mlstm_m)
```

```python
import functools

import numpy as np
import jax
import jax.numpy as jnp
from jax import lax
from jax.experimental import pallas as pl
from jax.experimental.pallas import tpu as pltpu

F32 = jnp.float32
BF16 = jnp.bfloat16

D_MODEL = 1024
BATCH, SEQ = 16, 256
DEC_BATCH, DEC_SEQ = 2, 2048
GRID_W = 64
N_P = BATCH * SEQ
N_S = DEC_BATCH * DEC_SEQ
N_TOK = N_P + N_S
D_A, DK_A, H_A = 1024, 128, 8
D_B, HS_B, H_B = 1024, 64, 16
R_LOW = 64
D_C, H_C, DH_C = 2048, 4, 512
SHIFT_W_B = 3 * D_B + 4 * R_LOW
IN_EVEN = 5 * D_A + SHIFT_W_B + D_B
IN_EVEN_PAD = 9728
IN_ODD_MAIN = 5 * D_C
EPS = 1e-6
GN_EPS_B = 64e-5

ROW_BLK = 256
SEQ_BLK = 2048
L_HGRN = 64
HGRN_HEADS = 4
L_RWKV = 32
SB_RWKV = 128
NSB_RWKV = 2
RWKV_VMEM_BYTES = 48 * 1024 * 1024
L_MLSTM = 256
LANE = 128

NN = (((1,), (0,)), ((), ()))
NT = (((1,), (1,)), ((), ()))
TN = (((0,), (0,)), ((), ()))


def _dot(a, b, dims=NN):
    return lax.dot_general(a.astype(BF16), b.astype(BF16), dims, preferred_element_type=F32)


def _split2(x):
    hi = x.astype(BF16)
    lo = (x - hi.astype(F32)).astype(BF16)
    return hi, lo


def _dot_sel(g, x):
    hi, lo = _split2(x)
    return (lax.dot_general(g, hi, NN, preferred_element_type=F32)
            + lax.dot_general(g, lo, NN, preferred_element_type=F32))


def _dot_sel_r(x, g):
    hi, lo = _split2(x)
    return (lax.dot_general(hi, g, NN, preferred_element_type=F32)
            + lax.dot_general(lo, g, NN, preferred_element_type=F32))


def _sigmoid(x):
    return 1.0 / (1.0 + jnp.exp(-x))


def _silu(x):
    return x * _sigmoid(x)


def _logsig(x):
    return jnp.minimum(x, 0.0) - jnp.log(1.0 + jnp.exp(-jnp.abs(x)))


def _hgrn_consts(L):
    nlev = int(np.log2(L))
    t = np.arange(L)
    G = np.zeros((nlev + 2, L, L), np.float32)
    G[0] = t[None, :] <= t[:, None]
    G[1] = t[None, :] > t[:, None]
    M = np.zeros((nlev + 1, L, L), np.float32)
    M[0] = np.eye(L)
    for lev in range(nlev):
        m = 1 << lev
        grp, pos = t // (2 * m), t % (2 * m)
        anchor = grp * 2 * m + m - 1
        for r in range(L):
            if pos[r] >= m:
                G[2 + lev, r, anchor[r] + 1:r + 1] = 1.0
            else:
                G[2 + lev, r, r + 1:anchor[r] + 1] = 1.0
        M[1 + lev] = (grp[:, None] == grp[None, :]) & (pos[:, None] >= m) & (pos[None, :] < m)
    G2 = np.stack([G.reshape(-1, L), G[:, ::-1, ::-1].reshape(-1, L)])
    M2 = np.stack([M, M[:, ::-1, ::-1]])
    return jnp.asarray(G2, BF16), jnp.asarray(M2, F32)


def _tri_consts(L):
    t = np.arange(L)
    return np.stack([t[None, :] <= t[:, None], t[None, :] >= t[:, None]]).astype(np.float32)


def _head_sum_const():
    i = np.arange(LANE)
    return jnp.asarray((i[:, None] // HS_B) == (i[None, :] // HS_B), BF16)


def _hgrn_chunks(probs, M_ref):
    L = probs[0][0].shape[0]
    nlev = M_ref.shape[1] - 1
    st = []
    for (q, v, pre, lb, St, G, d) in probs:
        f = lb + (1.0 - lb) * _sigmoid(pre)
        logf = jnp.log(f)
        st.append(dict(q=q, v=v, St=St, G=G, d=d, logf=logf, k=1.0 - f))
    for p in st:
        p["E"] = _dot_sel(p["G"], p["logf"])
    for p in st:
        p["o"] = _dot(p["q"] * jnp.exp(p["E"][0:L]), p["St"], NT)
    for p in st:
        p["att"] = M_ref[p["d"], 0] * _dot(p["q"], p["k"], NT)
    for lev in range(nlev):
        for p in st:
            F = jnp.exp(p["E"][(2 + lev) * L:(3 + lev) * L])
            p["att"] = p["att"] + M_ref[p["d"], 1 + lev] * _dot(p["q"] * F, p["k"] * F, NT)
    for p in st:
        p["o"] = p["o"] + _dot(p["att"], p["v"])
    out = []
    for p in st:
        tot = jnp.sum(p["logf"], axis=0, keepdims=True)
        St = jnp.exp(tot) * p["St"] + _dot(p["v"], p["k"] * jnp.exp(p["E"][L:2 * L]), TN)
        out.append((p["o"], St))
    return out


def _rwkv_consts(L, SB):
    t = np.arange(SB)
    same = (t[:, None] // L) == (t[None, :] // L)
    le = t[None, :] <= t[:, None]
    ge = t[None, :] >= t[:, None]
    lt = t[None, :] < t[:, None]
    gt = t[None, :] > t[:, None]
    sel = np.stack([np.concatenate([same & le, same & gt]), np.concatenate([same & ge, same & lt])])
    strict = np.stack([same & lt, same & gt])
    incl = np.stack([same & le, same & ge])
    return jnp.asarray(sel, BF16), jnp.asarray(strict, F32), jnp.asarray(incl, F32)


def _swap_heads(x):
    return pltpu.roll(x, HS_B, 1)


def _rwkv_block(L, r, lw, kd, v, vsw, kk, bv, sel, strict, incl, lm0, lm1):
    return _rwkv_blocks(L, [(r, lw, kd, v, vsw, kk, bv, sel, strict, incl)], lm0, lm1)[0]


def _rwkv_blocks(L, probs, lm0, lm1):
    SB = probs[0][0].shape[0]
    st = []
    for (r, lw, kd, v, vsw, kk, bv, sel, strict, incl) in probs:
        gg = _dot_sel(sel, lw)
        st.append(dict(r=r, lw=lw, kd=kd, v=v, vsw=vsw, kk=kk, bv=bv, strict=strict, incl=incl,
                       ginc=gg[:SB], gend=gg[SB:]))
    for p in st:
        e_inc = jnp.exp(p["ginc"])
        e_exc = jnp.exp(p["ginc"] - p["lw"])
        e_neg = jnp.exp(-p["ginc"])
        e_end = jnp.exp(p["gend"])
        p["Kq"], p["Rq"] = p["kk"] * e_exc, p["r"] * e_inc
        Bi, Ki = p["bv"] * e_neg, p["kd"] * e_neg
        p["Be"], p["Ke"] = p["bv"] * e_end, p["kd"] * e_end
        p["lhs"] = jnp.concatenate([p["Kq"] * lm0, p["Kq"] * lm1, p["Rq"] * lm0, p["Rq"] * lm1], axis=0)
        p["rhs"] = jnp.concatenate([Bi, Ki], axis=0)
    for p in st:
        p["sc"] = _dot(p["lhs"], p["rhs"], NT)
    ch = []
    for p in st:
        for h in (0, 1):
            sc = p["sc"]
            ch.append(dict(p=p, h=h,
                           Mb=p["strict"] * sc[h * SB:(h + 1) * SB, :SB],
                           Mk=p["strict"] * sc[h * SB:(h + 1) * SB, SB:],
                           Nb=p["incl"] * sc[(2 + h) * SB:(3 + h) * SB, :SB],
                           Nk=p["incl"] * sc[(2 + h) * SB:(3 + h) * SB, SB:]))
    for c in ch:
        c["MkV"] = _dot(c["Mk"], c["p"]["vsw"])
    for c in ch:
        lmh, lmo = (lm0, lm1) if c["h"] == 0 else (lm1, lm0)
        c["X"] = lmh * c["p"]["Kq"] + lmo * c["MkV"]
        c["Mp"] = c["Mb"]
    for c in ch:
        c["X"] = c["X"] - _dot(c["Mb"], c["X"])
    n = 2
    while n < L:
        for c in ch:
            c["Mp"] = _dot(c["Mp"], c["Mp"])
        for c in ch:
            c["X"] = c["X"] + _dot(c["Mp"], c["X"])
        n *= 2
    for c in ch:
        c["G"] = _dot(c["Nb"], c["X"])
    for c in ch:
        c["NkV"] = _dot(c["Nk"], c["p"]["v"])
    out = []
    for i, p in enumerate(st):
        c0, c1 = ch[2 * i], ch[2 * i + 1]
        A1 = lm0 * c0["X"] + lm1 * c1["X"]
        A2 = _swap_heads(lm0 * c1["X"] + lm1 * c0["X"])
        Qe = p["Rq"] - (lm0 * c0["G"] + lm1 * c1["G"])
        Yl = lm0 * c0["NkV"] + lm1 * c1["NkV"] - _swap_heads(lm0 * c1["G"] + lm1 * c0["G"])
        out.append((Qe, Yl, A1, A2, p["Be"], p["Ke"], p["ginc"] + p["gend"]))
    return out


def _mlstm_chunk(q, k, v, gi_col, gf_col, gi_row, gf_row, C, n, m, tri, tri_t, causal):
    L = q.shape[0]
    lf_col = _logsig(gf_col)
    lf_row = _logsig(gf_row)
    b_col = _dot_sel(tri, jnp.broadcast_to(lf_col, (L, LANE)))[:, 0:1]
    b_row = _dot_sel_r(jnp.broadcast_to(lf_row, (8, L)), tri_t)[0:1, :]
    btot = jnp.sum(lf_col, axis=0, keepdims=True)
    log_d = jnp.where(causal > 0, b_col - b_row + gi_row, -jnp.inf)
    m_loc = jnp.max(log_d, axis=1, keepdims=True)
    log_prev = b_col + m
    m_t = jnp.maximum(log_prev, m_loc)
    w_prev = jnp.exp(log_prev - m_t)
    s = _dot(q, k, NT) * jnp.exp(log_d - m_t)
    num = w_prev * _dot(q, C) + _dot(s, v)
    den = w_prev * jnp.sum(q * n, axis=1, keepdims=True) + jnp.sum(s, axis=1, keepdims=True)
    h = num / jnp.maximum(jnp.abs(den), jnp.exp(-m_t))
    log_s = btot - b_col + gi_col
    m_new = jnp.maximum(btot + m, jnp.max(log_s, axis=0, keepdims=True))
    kw = k * jnp.exp(log_s - m_new)
    w_old = jnp.exp(btot + m - m_new)
    C = w_old * C + _dot(kw, v, TN)
    n = w_old * n + jnp.sum(kw, axis=0, keepdims=True)
    return h, C, n, m_new


def _adaln_kernel(c_ref, w_ref, b_ref, o_ref):
    o_ref[0] = _dot(_silu(c_ref[...]), w_ref[0]) + b_ref[0]


def _adaln(cond8, w_mod, b_mod):
    depth, _, n3 = w_mod.shape
    tn = 1024
    return pl.pallas_call(
        _adaln_kernel,
        grid=(depth, n3 // tn),
        in_specs=[pl.BlockSpec((8, D_MODEL), lambda l, j: (0, 0)),
                  pl.BlockSpec((1, D_MODEL, tn), lambda l, j: (l, 0, j)),
                  pl.BlockSpec((1, 1, tn), lambda l, j: (l, 0, j))],
        out_specs=pl.BlockSpec((1, 8, tn), lambda l, j: (l, 0, j)),
        out_shape=jax.ShapeDtypeStruct((depth, 8, n3), F32),
        name="adaln",
    )(cond8, w_mod, b_mod.reshape(depth, 1, n3))


def _rms_mod(x, g, mod):
    y = x * lax.rsqrt(jnp.mean(x * x, axis=-1, keepdims=True) + EPS) * g
    return y * (1.0 + mod[:, D_MODEL:2 * D_MODEL]) + mod[:, 0:D_MODEL]


def _norm_mod_kernel(x_ref, g_ref, mod_ref, h_ref):
    h_ref[...] = _rms_mod(x_ref[...], g_ref[...], mod_ref[0]).astype(BF16)


def _norm_mod(x, g, mod_rows):
    nblk = x.shape[0] // ROW_BLK
    return pl.pallas_call(
        _norm_mod_kernel,
        grid=(nblk,),
        in_specs=[pl.BlockSpec((ROW_BLK, D_MODEL), lambda i: (i, 0)),
                  pl.BlockSpec((1, D_MODEL), lambda i: (0, 0)),
                  pl.BlockSpec((1, 1, 3 * D_MODEL), lambda i: (i, 0, 0))],
        out_specs=pl.BlockSpec((ROW_BLK, D_MODEL), lambda i: (i, 0)),
        out_shape=jax.ShapeDtypeStruct(x.shape, BF16),
        name="norm_mod",
    )(x, g, mod_rows)


def _matmul_kernel(a_ref, w_ref, o_ref):
    o_ref[...] = jnp.dot(a_ref[...], w_ref[...], preferred_element_type=F32)


def _matmul(a, w, tm=1024, tn=512):
    m, k = a.shape
    n = w.shape[1]
    tn = min(tn, n)
    tiles = 2 * (tm * k * 2 + k * tn * 2 + tm * tn * 4)
    return pl.pallas_call(
        _matmul_kernel,
        grid=(m // tm, n // tn),
        in_specs=[pl.BlockSpec((tm, k), lambda i, j: (i, 0)),
                  pl.BlockSpec((k, tn), lambda i, j: (0, j))],
        out_specs=pl.BlockSpec((tm, tn), lambda i, j: (i, j)),
        out_shape=jax.ShapeDtypeStruct((m, n), F32),
        compiler_params=pltpu.CompilerParams(vmem_limit_bytes=tiles + tiles // 4),
        name="in_proj",
    )(a, w)


def _add_prev(in_specs, args, prev):
    if prev is None:
        return {}
    in_specs.append(pl.BlockSpec(memory_space=pl.ANY))
    args.append(prev)
    return {len(args) - 1: 0}


def _hgrn_kernel(T, has_state, has_prev, want_final, *refs):
    q_ref, v_ref, ff_ref, fb_ref, z_ref, lb_ref, gn_ref, G_ref, M_ref = refs[:9]
    pos = 9
    s0_ref = None
    if has_state:
        s0_ref = refs[pos]
        pos += 1
    pos += int(has_prev)
    out_ref = refs[pos]
    pos += 1
    sfin_ref = None
    if want_final:
        sfin_ref = refs[pos]
        pos += 1
    o_dir = refs[pos]
    L = L_HGRN
    nc = T // L
    f_refs = (ff_ref, fb_ref)
    chains = [(hh, d) for hh in range(HGRN_HEADS) for d in (0, 1)]
    lanes = lambda hh: slice(hh * DK_A, (hh + 1) * DK_A)

    def step(c, carry):
        probs, where = [], []
        for i, (hh, d) in enumerate(chains):
            ci = c if d == 0 else nc - 1 - c
            rows = pl.ds(pl.multiple_of(ci * L, L), L)
            probs.append((q_ref[rows, lanes(hh)], v_ref[rows, lanes(hh)], f_refs[d][rows, lanes(hh)],
                          lb_ref[hh], carry[i], G_ref[d], d))
            where.append(rows)
        res = _hgrn_chunks(probs, M_ref)
        for (hh, d), rows, (o, _) in zip(chains, where, res):
            o_dir[d, rows, lanes(hh)] = o
        return tuple(st for _, st in res)

    if has_state:
        init = tuple(s0_ref[0, d, hh] for hh, d in chains)
    else:
        init = tuple(jnp.zeros((DK_A, DK_A), F32) for _ in chains)
    fin = lax.fori_loop(0, nc, step, init)
    if want_final:
        for (hh, d), st in zip(chains, fin):
            sfin_ref[0, d, hh] = st
    for hh in range(HGRN_HEADS):
        o = o_dir[0, :, lanes(hh)] + o_dir[1, :, lanes(hh)]
        y = o * lax.rsqrt(jnp.mean(o * o, axis=-1, keepdims=True) + EPS) * gn_ref[hh]
        out_ref[:, lanes(hh)] = (y * _silu(z_ref[:, lanes(hh)])).astype(BF16)


def _hgrn(proj, lb, gn, consts, prev_out, nb, T, row0, s0_t=None, want_final=False):
    G, M = consts
    rb0 = row0 // T
    nh = HGRN_HEADS
    w = nh * DK_A
    col = lambda off: pl.BlockSpec((T, w), lambda b, h, off=off: (rb0 + b, off // nh + h))
    in_specs = [col(0), col(8), col(16), col(24), col(32),
                pl.BlockSpec((nh, 1, LANE), lambda b, h: (h, 0, 0)),
                pl.BlockSpec((nh, 1, LANE), lambda b, h: (h, 0, 0)),
                pl.BlockSpec(G.shape, lambda b, h: (0, 0, 0)),
                pl.BlockSpec(M.shape, lambda b, h: (0, 0, 0, 0))]
    args = [proj, proj, proj, proj, proj, lb, gn, G, M]
    if s0_t is not None:
        in_specs.append(pl.BlockSpec((1, 2, nh, DK_A, DK_A), lambda b, h: (b, 0, h, 0, 0)))
        args.append(s0_t)
    aliases = _add_prev(in_specs, args, prev_out)
    out_specs = [pl.BlockSpec((T, w), lambda b, h: (rb0 + b, h))]
    out_shape = [jax.ShapeDtypeStruct((N_TOK, D_A), BF16)]
    if want_final:
        out_specs.append(pl.BlockSpec((1, 2, nh, DK_A, DK_A), lambda b, h: (b, 0, h, 0, 0)))
        out_shape.append(jax.ShapeDtypeStruct((nb, 2, H_A, DK_A, DK_A), F32))
    res = pl.pallas_call(
        functools.partial(_hgrn_kernel, T, s0_t is not None, prev_out is not None, want_final),
        grid=(nb, H_A // nh),
        in_specs=in_specs,
        out_specs=out_specs,
        out_shape=out_shape,
        scratch_shapes=[pltpu.VMEM((2, T, w), F32)],
        input_output_aliases=aliases,
        name="hgrn2_T%d" % T,
    )(*args)
    return res


def _seq_pos(shape):
    is_latent = pl.program_id(0) >= N_P // SEQ_BLK
    seq_len = jnp.where(is_latent, DEC_SEQ, SEQ)
    t = lax.broadcasted_iota(jnp.int32, shape, 0)
    return jnp.bitwise_and(t, seq_len - 1), seq_len, is_latent


def _shift_kernel(p_ref, mu_ref, o_ref):
    p = p_ref[...]
    pos, seq_len, _ = _seq_pos(p.shape)
    prev = jnp.where(pos == 0, 0.0, pltpu.roll(p, 1, 0))
    nxt = jnp.where(pos == seq_len - 1, 0.0, pltpu.roll(p, p.shape[0] - 1, 0))
    mu = mu_ref[...]
    o_ref[...] = p + mu[0:1] * (prev - p) + mu[1:2] * (nxt - p)


def _shift(proj, mu):
    tc = 256
    c0 = (5 * D_A) // tc
    return pl.pallas_call(
        _shift_kernel,
        grid=(N_TOK // SEQ_BLK, SHIFT_W_B // tc),
        in_specs=[pl.BlockSpec((SEQ_BLK, tc), lambda i, j: (i, c0 + j)),
                  pl.BlockSpec((2, tc), lambda i, j: (0, j))],
        out_specs=pl.BlockSpec((SEQ_BLK, tc), lambda i, j: (i, j)),
        out_shape=jax.ShapeDtypeStruct((N_TOK, SHIFT_W_B), F32),
        name="rwkv_shift",
    )(proj, mu)


def _rwkv_prep_kernel(r_ref, k_ref, v_ref, low_ref, w0_ref, w2_ref, a0_ref, a2_ref, kk_ref_p, ka_ref, rk_ref,
                      bd_ref, kk_ref, lw_ref, kd_ref, bv_ref, bonus_ref):
    r, k, v, low = r_ref[...], k_ref[...], v_ref[...], low_ref[...]
    bd = bd_ref[...]
    tl = jnp.tanh(low)
    kkr = k * kk_ref_p[...]
    kk = kkr / jnp.maximum(jnp.sqrt(_dot_sel_r(kkr * kkr, bd)), 1e-12)
    kk_ref[...] = kk
    bonus = jnp.zeros_like(r)
    for d in (0, 1):
        u = w0_ref[d] + _dot(tl, w2_ref[d])
        lw_ref[d] = -np.float32(np.exp(-0.5)) * _sigmoid(u)
        a = _sigmoid(a0_ref[d] + _dot(low, a2_ref[d]))
        kd = k * (1.0 + (a - 1.0) * ka_ref[...])
        kd_ref[d] = kd
        bv_ref[d] = a * kk
        bonus = bonus + _dot_sel_r(r * kd * rk_ref[...], bd) * v
    bonus_ref[...] = bonus


def _rwkv_prep(sh, w0, w2p, a0, a2p, k_k, k_a, r_k, bd, tm=1024):
    n = sh.shape[0]
    tile = lambda off: pl.BlockSpec((tm, LANE), lambda i, j, off=off: (i, off + j))
    vec = pl.BlockSpec((1, LANE), lambda i, j: (0, j))
    dvec = pl.BlockSpec((2, 1, LANE), lambda i, j: (0, 0, j))
    dmat = pl.BlockSpec((2, 4 * R_LOW, LANE), lambda i, j: (0, 0, j))
    dout = pl.BlockSpec((2, tm, LANE), lambda i, j: (0, i, j))
    one = jax.ShapeDtypeStruct((n, D_B), F32)
    two = jax.ShapeDtypeStruct((2, n, D_B), F32)
    return pl.pallas_call(
        _rwkv_prep_kernel,
        grid=(n // tm, D_B // LANE),
        in_specs=[tile(0), tile(8), tile(16),
                  pl.BlockSpec((tm, 4 * R_LOW), lambda i, j: (i, (3 * D_B) // (4 * R_LOW))),
                  dvec, dmat, dvec, dmat, vec, vec, vec,
                  pl.BlockSpec((LANE, LANE), lambda i, j: (0, 0))],
        out_specs=[tile(0), dout, dout, dout, tile(0)],
        out_shape=[one, two, two, two, one],
        name="rwkv_prep",
    )(sh, sh, sh, sh, w0, w2p, a0, a2p, k_k, k_a, r_k, bd)


def _rwkv_kernel(T, has_state, has_prev, want_final, *refs):
    (r_ref, v_ref, kk_ref, lw_ref, kd_ref, bv_ref, sel_ref, strict_ref, incl_ref, bdeye_ref) = refs[:10]
    pos = 10
    s0_ref = None
    if has_state:
        s0_ref = refs[pos]
        pos += 1
    pos += int(has_prev)
    y_ref = refs[pos]
    pos += 1
    sfin_ref = None
    if want_final:
        sfin_ref = refs[pos]
        pos += 1
    tm_scr, u_scr, qe_scr, s_scr = refs[pos:pos + 4]
    L, SB = L_RWKV, SB_RWKV
    nc = T // L
    cps = SB // L

    def block_step(it, _):
        lane = lax.broadcasted_iota(jnp.int32, (1, LANE), 1)
        lm0 = (lane < HS_B).astype(F32)
        lm1 = 1.0 - lm0
        probs, keys, vs = [], [], {}
        for s in range(NSB_RWKV):
            sb = it * NSB_RWKV + s
            rows = pl.ds(pl.multiple_of(sb * SB, SB), SB)
            r, v, kk = r_ref[rows, :], v_ref[rows, :], kk_ref[rows, :]
            vsw = _swap_heads(v)
            vs[s] = v
            for d in (0, 1):
                probs.append((r, lw_ref[d, rows, :], kd_ref[d, rows, :], v, vsw, kk, bv_ref[d, rows, :],
                              sel_ref[d], strict_ref[d], incl_ref[d]))
                keys.append((s, sb, rows, d))
        res = _rwkv_blocks(L, probs, lm0, lm1)
        for (s, sb, rows, d), out in zip(keys, res):
            y_ref[d, rows, :] = out[1]
            qe_scr[d, rows, :] = out[0].astype(BF16)
        for j in range(cps):
            rc = slice(j * L, (j + 1) * L)
            for (s, sb, rows, d), (_, _, A1, A2, Be, Ke, gt) in zip(keys, res):
                decay = bdeye_ref[1] * jnp.exp(gt[j * L:j * L + 1])
                tm_scr[d, sb * cps + j] = (decay - bdeye_ref[0] * _dot(A1[rc], Be[rc], TN)).astype(BF16)
            for (s, sb, rows, d), (_, _, A1, A2, Be, Ke, gt) in zip(keys, res):
                u_scr[d, sb * cps + j] = bdeye_ref[0] * _dot(
                    jnp.concatenate([vs[s][rc], -A2[rc]], axis=0), jnp.concatenate([Ke[rc], Be[rc]], axis=0), TN)
        return 0

    lax.fori_loop(0, T // (SB * NSB_RWKV), block_step, 0)

    def chunk_step(c, _):
        for d in (0, 1):
            ci = c if d == 0 else nc - 1 - c
            rows = pl.ds(pl.multiple_of(ci * L, L), L)
            S = s_scr[d].astype(BF16)
            y_ref[d, rows, :] = y_ref[d, rows, :] + lax.dot_general(qe_scr[d, rows, :], S, NT,
                                                                    preferred_element_type=F32)
            s_scr[d] = lax.dot_general(S, tm_scr[d, ci], NN, preferred_element_type=F32) + u_scr[d, ci]
        return 0

    zero = jnp.zeros((HS_B, HS_B), F32)
    for d in (0, 1):
        if has_state:
            top = jnp.concatenate([s0_ref[0, d, 0], zero], axis=1)
            bot = jnp.concatenate([zero, s0_ref[0, d, 1]], axis=1)
            s_scr[d] = jnp.concatenate([top, bot], axis=0)
        else:
            s_scr[d] = jnp.zeros((LANE, LANE), F32)
    lax.fori_loop(0, nc, chunk_step, 0)
    if want_final:
        for d in (0, 1):
            fin = s_scr[d]
            sfin_ref[0, d, 0] = fin[0:HS_B, 0:HS_B]
            sfin_ref[0, d, 1] = fin[HS_B:LANE, HS_B:LANE]


def _rwkv(sh, kk, lw, kd, bv, consts, prev_y, nb, T, row0, s0=None, want_final=False):
    sel, strict, incl, bdeye = consts
    rb0 = row0 // T
    nhp = H_B // 2
    nc = T // L_RWKV
    tok = lambda off: pl.BlockSpec((T, LANE), lambda b, p, off=off: (rb0 + b, off + p))
    dtok = pl.BlockSpec((2, T, LANE), lambda b, p: (0, rb0 + b, p))
    cst = lambda a: pl.BlockSpec(a.shape, lambda b, p: (0, 0, 0))
    st = pl.BlockSpec((1, 2, 2, HS_B, HS_B), lambda b, p: (b, 0, p, 0, 0))
    in_specs = [tok(0), tok(2 * D_B // LANE), tok(0), dtok, dtok, dtok, cst(sel), cst(strict), cst(incl), cst(bdeye)]
    args = [sh, sh, kk, lw, kd, bv, sel, strict, incl, bdeye]
    if s0 is not None:
        in_specs.append(st)
        args.append(s0)
    aliases = _add_prev(in_specs, args, prev_y)
    out_specs = [dtok]
    out_shape = [jax.ShapeDtypeStruct((2, N_TOK, D_B), F32)]
    if want_final:
        out_specs.append(st)
        out_shape.append(jax.ShapeDtypeStruct((nb, 2, H_B, HS_B, HS_B), F32))
    return pl.pallas_call(
        functools.partial(_rwkv_kernel, T, s0 is not None, prev_y is not None, want_final),
        grid=(nb, nhp),
        in_specs=in_specs,
        out_specs=out_specs,
        out_shape=out_shape,
        scratch_shapes=[pltpu.VMEM((2, nc, LANE, LANE), BF16), pltpu.VMEM((2, nc, LANE, LANE), F32),
                        pltpu.VMEM((2, T, LANE), BF16), pltpu.VMEM((2, LANE, LANE), F32)],
        input_output_aliases=aliases,
        compiler_params=pltpu.CompilerParams(vmem_limit_bytes=RWKV_VMEM_BYTES),
        name="rwkv7_T%d" % T,
    )(*args)


def _rwkv_combine_kernel(y_ref, bonus_ref, z_ref, g_ref, b_ref, bd_ref, o_ref):
    bd = bd_ref[...]
    y = y_ref[0] + y_ref[1]
    yc = y - _dot_sel_r(y, bd) * (1.0 / HS_B)
    var = _dot_sel_r(yc * yc, bd) * (1.0 / HS_B)
    yn = yc * lax.rsqrt(var + GN_EPS_B) * g_ref[...] + b_ref[...]
    o_ref[...] = ((yn + bonus_ref[...]) * _silu(z_ref[...])).astype(BF16)


def _rwkv_combine(y2, bonus, proj, gn_g, gn_b, bd, tm=1024):
    n = bonus.shape[0]
    zc0 = (5 * D_A + SHIFT_W_B) // LANE
    tile = pl.BlockSpec((tm, LANE), lambda i, j: (i, j))
    vec = pl.BlockSpec((1, LANE), lambda i, j: (0, j))
    return pl.pallas_call(
        _rwkv_combine_kernel,
        grid=(n // tm, D_B // LANE),
        in_specs=[pl.BlockSpec((2, tm, LANE), lambda i, j: (0, i, j)), tile,
                  pl.BlockSpec((tm, LANE), lambda i, j: (i, zc0 + j)), vec, vec,
                  pl.BlockSpec((LANE, LANE), lambda i, j: (0, 0))],
        out_specs=tile,
        out_shape=jax.ShapeDtypeStruct((n, D_B), BF16),
        name="rwkv_combine",
    )(y2, bonus, proj, gn_g, gn_b, bd)


def _out_proj_kernel(final, a_ref, b_ref, w_ref, x_ref, mod_ref, g_ref, *rest):
    acc = jnp.dot(a_ref[...], w_ref[0:D_MODEL, :], preferred_element_type=F32)
    acc = acc + jnp.dot(b_ref[...], w_ref[D_MODEL:2 * D_MODEL, :], preferred_element_type=F32)
    x = x_ref[...] + mod_ref[0][:, 2 * D_MODEL:3 * D_MODEL] * acc
    if final:
        (y_ref,) = rest
        y_ref[...] = x * lax.rsqrt(jnp.mean(x * x, axis=-1, keepdims=True) + EPS) * g_ref[...]
    else:
        nmod_ref, x_out_ref, h_ref = rest
        x_out_ref[...] = x
        h_ref[...] = _rms_mod(x, g_ref[...], nmod_ref[0]).astype(BF16)


def _out_proj(a, b, a_col, b_col, w, x, mod_rows, g, next_mod_rows=None):
    n = x.shape[0]
    nblk = n // ROW_BLK
    final = next_mod_rows is None
    row = lambda c: pl.BlockSpec((ROW_BLK, D_MODEL), lambda i, c=c: (i, c))
    modspec = pl.BlockSpec((1, 1, 3 * D_MODEL), lambda i: (i, 0, 0))
    in_specs = [row(a_col), row(b_col), pl.BlockSpec(w.shape, lambda i: (0, 0)), row(0), modspec,
                pl.BlockSpec((1, D_MODEL), lambda i: (0, 0))]
    args = [a, b, w, x, mod_rows, g]
    if final:
        out_specs = row(0)
        out_shape = jax.ShapeDtypeStruct((n, D_MODEL), F32)
    else:
        in_specs.append(modspec)
        args.append(next_mod_rows)
        out_specs = [row(0), row(0)]
        out_shape = [jax.ShapeDtypeStruct((n, D_MODEL), F32), jax.ShapeDtypeStruct((n, D_MODEL), BF16)]
    return pl.pallas_call(
        functools.partial(_out_proj_kernel, final),
        grid=(nblk,),
        in_specs=in_specs,
        out_specs=out_specs,
        out_shape=out_shape,
        name="out_proj_final" if final else "out_proj",
    )(*args)


def _conv_taps(x_ref, w_ref, b_ref, o_ref, two_d, seq_len):
    n, tc = x_ref.shape
    t = lax.broadcasted_iota(jnp.int32, (n, LANE), 0)
    pos = jnp.bitwise_and(t, seq_len - 1)
    line = GRID_W if two_d else seq_len
    col = jnp.bitwise_and(pos, line - 1)
    not_last = (col != line - 1).astype(F32)
    not_first = (col != 0).astype(F32)
    if two_d:
        r = jnp.right_shift(pos, int(np.log2(GRID_W)))
        has_up = (r != 0).astype(F32)
        has_down = (r != seq_len // GRID_W - 1).astype(F32)
    for j in range(tc // LANE):
        cs = slice(j * LANE, (j + 1) * LANE)
        x = x_ref[:, cs]
        left = pltpu.roll(x * not_last, 1, 0)
        right = pltpu.roll(x * not_first, n - 1, 0)
        wrow = lambda i: (w_ref[3 * i:3 * i + 1, cs] * left + w_ref[3 * i + 1:3 * i + 2, cs] * x
                          + w_ref[3 * i + 2:3 * i + 3, cs] * right)
        acc = wrow(1) + b_ref[:, cs]
        if two_d:
            acc = acc + has_up * pltpu.roll(wrow(0), GRID_W, 0) + has_down * pltpu.roll(wrow(2), n - GRID_W, 0)
        o_ref[:, cs] = _silu(acc)


def _conv_kernel(x_ref, w_ref, b_ref, o_ref):
    is_latent = pl.program_id(0) >= N_P // SEQ_BLK

    @pl.when(jnp.logical_not(is_latent))
    def _():
        _conv_taps(x_ref, w_ref, b_ref, o_ref, False, SEQ)

    @pl.when(is_latent)
    def _():
        _conv_taps(x_ref, w_ref, b_ref, o_ref, True, DEC_SEQ)


def _conv(proj, w9, bias, tc=256):
    return pl.pallas_call(
        _conv_kernel,
        grid=(N_TOK // SEQ_BLK, (2 * D_C) // tc),
        in_specs=[pl.BlockSpec((SEQ_BLK, tc), lambda i, j: (i, j)),
                  pl.BlockSpec((9, tc), lambda i, j: (0, j)),
                  pl.BlockSpec((1, tc), lambda i, j: (0, j))],
        out_specs=pl.BlockSpec((SEQ_BLK, tc), lambda i, j: (i, j)),
        out_shape=jax.ShapeDtypeStruct((N_TOK, 2 * D_C), F32),
        name="mlstm_conv",
    )(proj, w9, bias)


def _mlstm_kernel(nc, has_state, has_prev, want_final, *refs):
    (q_ref, k_ref, v_ref, ic_ref, fc_ref, ir_ref, fr_ref, bi_ref, bf_ref,
     tri_ref, trit_ref, causal_ref) = refs[:12]
    pos = 12
    if has_state:
        c0_ref, n0_ref, m0_ref = refs[pos:pos + 3]
        pos += 3
    pos += int(has_prev)
    h_ref = refs[pos]
    pos += 1
    if want_final:
        cf_ref, nf_ref, mf_ref = refs[pos:pos + 3]
        pos += 3
    c_scr, n_scr, m_scr = refs[pos:pos + 3]
    c = pl.program_id(3)

    @pl.when(c == 0)
    def _():
        if has_state:
            c_scr[...] = c0_ref[0, 0, 0]
            n_scr[...] = n0_ref[0, 0, 0]
            m_scr[...] = m0_ref[0, 0, 0]
        else:
            c_scr[...] = jnp.zeros_like(c_scr)
            n_scr[...] = jnp.zeros_like(n_scr)
            m_scr[...] = jnp.zeros_like(m_scr)

    bi, bf = bi_ref[0, 0], bf_ref[0, 0]
    k = k_ref[...] * np.float32(DH_C ** -0.5)
    h, C, n, m = _mlstm_chunk(q_ref[...], k, v_ref[...],
                              ic_ref[0, 0] + bi, fc_ref[0, 0] + bf, ir_ref[0, 0] + bi, fr_ref[0, 0] + bf,
                              c_scr[...], n_scr[...], m_scr[...],
                              tri_ref[0], trit_ref[0], causal_ref[0])
    h_ref[0] = h
    c_scr[...] = C
    n_scr[...] = n
    m_scr[...] = m
    if want_final:
        @pl.when(c == nc - 1)
        def _():
            cf_ref[0, 0, 0] = C
            nf_ref[0, 0, 0] = n
            mf_ref[0, 0, 0] = m


def _mlstm(qk, proj, gates, consts, prev_h, nb, T, row0, state=None, want_final=False):
    ic, fc, ir, fr, bi, bf = gates
    tri, trit, causal = consts
    L = L_MLSTM
    nc = T // L
    rb0 = row0 // L
    blk = lambda b, d, c: rb0 + b * nc + c + d * (nc - 1 - 2 * c)
    tok = lambda off: pl.BlockSpec((L, DH_C), lambda b, h, d, c, off=off: (blk(b, d, c), off + h))
    gcol = pl.BlockSpec((1, 1, L, 1), lambda b, h, d, c: (d, h, blk(b, d, c), 0))
    grow = pl.BlockSpec((1, 1, 1, L), lambda b, h, d, c: (d, h, 0, blk(b, d, c)))
    gb = pl.BlockSpec((1, 1, 1, 1), lambda b, h, d, c: (d, h, 0, 0))
    cst = pl.BlockSpec((1, L, L), lambda b, h, d, c: (d, 0, 0))
    cs = pl.BlockSpec((1, 1, 1, DH_C, DH_C), lambda b, h, d, c: (b, d, h, 0, 0))
    ns = pl.BlockSpec((1, 1, 1, 1, DH_C), lambda b, h, d, c: (b, d, h, 0, 0))
    ms = pl.BlockSpec((1, 1, 1, 1, 1), lambda b, h, d, c: (b, d, h, 0, 0))
    in_specs = [tok(0), tok(H_C), tok(2 * H_C), gcol, gcol, grow, grow, gb, gb, cst, cst, cst]
    args = [qk, qk, proj, ic, fc, ir, fr, bi, bf, tri, trit, causal]
    if state is not None:
        in_specs += [cs, ns, ms]
        args += list(state)
    aliases = _add_prev(in_specs, args, prev_h)
    out_specs = [pl.BlockSpec((1, L, DH_C), lambda b, h, d, c: (d, blk(b, d, c), h))]
    out_shape = [jax.ShapeDtypeStruct((2, N_TOK, D_C), F32)]
    if want_final:
        out_specs += [cs, ns, ms]
        out_shape += [jax.ShapeDtypeStruct((nb, 2, H_C, DH_C, DH_C), F32),
                      jax.ShapeDtypeStruct((nb, 2, H_C, 1, DH_C), F32),
                      jax.ShapeDtypeStruct((nb, 2, H_C, 1, 1), F32)]
    return pl.pallas_call(
        functools.partial(_mlstm_kernel, nc, state is not None, prev_h is not None, want_final),
        grid=(nb, H_C, 2, nc),
        in_specs=in_specs,
        out_specs=out_specs,
        out_shape=out_shape,
        scratch_shapes=[pltpu.VMEM((DH_C, DH_C), F32), pltpu.VMEM((1, DH_C), F32), pltpu.VMEM((1, 1), F32)],
        input_output_aliases=aliases,
        name="mlstm_T%d" % T,
    )(*args)


def _mlstm_combine_kernel(h_ref, o_ref, z_ref, g_ref, y_ref):
    y = _sigmoid(o_ref[...]) * (h_ref[0] + h_ref[1])
    y = y * lax.rsqrt(jnp.mean(y * y, axis=-1, keepdims=True) + EPS) * g_ref[...]
    y_ref[...] = (y * _silu(z_ref[...])).astype(BF16)


def _mlstm_combine(h2, proj, norm_g, tm=512):
    n = proj.shape[0]
    return pl.pallas_call(
        _mlstm_combine_kernel,
        grid=(n // tm, H_C),
        in_specs=[pl.BlockSpec((2, tm, DH_C), lambda i, h: (0, i, h)),
                  pl.BlockSpec((tm, DH_C), lambda i, h: (i, 3 * H_C + h)),
                  pl.BlockSpec((tm, DH_C), lambda i, h: (i, 4 * H_C + h)),
                  pl.BlockSpec((1, DH_C), lambda i, h: (0, h))],
        out_specs=pl.BlockSpec((tm, DH_C), lambda i, h: (i, h)),
        out_shape=jax.ShapeDtypeStruct((n, D_C), BF16),
        name="mlstm_combine",
    )(h2, proj, proj, norm_g)


def kernel(x_prompt, x_sample, c, state_hgrn, state_rwkv, state_mlstm_C, state_mlstm_n, state_mlstm_m, c_ctx, w_mod, b_mod, norm_g, final_norm_g, w_in_even, w_out_even, hgrn_lb_logits, hgrn_norm_g, rwkv_shift_mu, rwkv_w0, rwkv_w2, rwkv_a0, rwkv_a2, rwkv_k_k, rwkv_k_a, rwkv_r_k, rwkv_gn_g, rwkv_gn_b, w_in_odd, w_out_odd, mlstm_conv_w, mlstm_conv_b, mlstm_gate_b, mlstm_norm_g):
    dt = x_prompt.dtype
    x = jnp.concatenate([x_prompt.reshape(N_P, D_MODEL), x_sample.reshape(N_S, D_MODEL)], axis=0)

    cond8 = jnp.zeros((8, D_MODEL), F32).at[0].set(c_ctx).at[1:1 + DEC_BATCH].set(c)
    mods = _adaln(cond8, w_mod, b_mod)
    blk_cond = np.concatenate([np.zeros(N_P // ROW_BLK, np.int32),
                               1 + np.repeat(np.arange(DEC_BATCH, dtype=np.int32), DEC_SEQ // ROW_BLK)])
    mod_rows = [mods[l][blk_cond][:, None, :] for l in range(2)]

    h0 = _norm_mod(x, norm_g[0:1], mod_rows[0])
    w_in0 = jnp.pad(w_in_even[0].astype(BF16), ((0, 0), (0, IN_EVEN_PAD - IN_EVEN)))
    proj = _matmul(h0, w_in0, tn=IN_EVEN_PAD // 4)

    lb_all = jnp.cumsum(jax.nn.softmax(hgrn_lb_logits.astype(F32), axis=0), axis=0)
    lb = lb_all[0].reshape(H_A, 1, DK_A)
    gn_a = hgrn_norm_g[0].reshape(H_A, 1, DK_A)
    hconsts = _hgrn_consts(L_HGRN)
    s_hgrn_t = jnp.swapaxes(state_hgrn[:, 0], -1, -2)
    out_a, sfin_a = _hgrn(proj, lb, gn_a, hconsts, None, BATCH, SEQ, 0, want_final=True)
    (out_a,) = _hgrn(proj, lb, gn_a, hconsts, out_a, DEC_BATCH, DEC_SEQ, N_P, s0_t=s_hgrn_t)
    new_hgrn = jnp.swapaxes(sfin_a, -1, -2)[:, None].astype(dt)

    mu = rwkv_shift_mu[0]
    sh = _shift(proj, mu)
    low_pad = lambda w, off: jnp.zeros((2, 4 * R_LOW, D_B), F32).at[0, off:off + R_LOW].set(w[0]).at[
        1, off + R_LOW:off + 2 * R_LOW].set(w[1]).astype(BF16)
    bd = _head_sum_const()
    kk, lw, kd, bv, bonus = _rwkv_prep(
        sh, rwkv_w0[0].reshape(2, 1, D_B), low_pad(rwkv_w2[0], 0), rwkv_a0[0].reshape(2, 1, D_B),
        low_pad(rwkv_a2[0], 2 * R_LOW), rwkv_k_k[0].reshape(1, D_B), rwkv_k_a[0].reshape(1, D_B),
        rwkv_r_k[0].reshape(1, D_B), bd)
    lane_head = np.arange(LANE) // HS_B
    bdeye = jnp.asarray(np.stack([lane_head[:, None] == lane_head[None, :], np.eye(LANE, dtype=bool)]), F32)
    rconsts = _rwkv_consts(L_RWKV, SB_RWKV) + (bdeye,)
    y2, sfin_b = _rwkv(sh, kk, lw, kd, bv, rconsts, None, BATCH, SEQ, 0, want_final=True)
    (y2,) = _rwkv(sh, kk, lw, kd, bv, rconsts, y2, DEC_BATCH, DEC_SEQ, N_P, s0=state_rwkv[:, 0])
    new_rwkv = sfin_b[:, None].astype(dt)
    out_b = _rwkv_combine(y2, bonus, proj, rwkv_gn_g[0].reshape(1, D_B), rwkv_gn_b[0].reshape(1, D_B), bd)

    x1, h1 = _out_proj(out_a, out_b, 0, 0, w_out_even[0].astype(BF16), x, mod_rows[0], norm_g[1:2],
                       next_mod_rows=mod_rows[1])

    w_in1 = w_in_odd[0].astype(BF16)
    proj1 = _matmul(h1, w_in1[:, :IN_ODD_MAIN], tn=IN_ODD_MAIN // 4)
    w_gate = jnp.pad(w_in1[:, IN_ODD_MAIN:], ((0, 0), (0, LANE - 4 * H_C)))
    gates = _matmul(h1, w_gate)[:, :4 * H_C]
    g4 = gates.reshape(N_TOK, 4, H_C).transpose(1, 2, 0)
    gate_args = (g4[0:2, :, :, None], g4[2:4, :, :, None], g4[0:2, :, None, :], g4[2:4, :, None, :],
                 mlstm_gate_b[0][0:2, :, None, None], mlstm_gate_b[0][2:4, :, None, None])

    w9 = mlstm_conv_w[0].reshape(9, 2 * D_C)
    cb = mlstm_conv_b[0].reshape(1, 2 * D_C)
    qk = _conv(proj1, w9, cb)

    tri_m = _tri_consts(L_MLSTM)
    mconsts = (jnp.asarray(tri_m, BF16), jnp.asarray(np.swapaxes(tri_m, 1, 2), BF16), jnp.asarray(tri_m))
    h2, cfin, nfin, mfin = _mlstm(qk, proj1, gate_args, mconsts, None, BATCH, SEQ, 0, want_final=True)
    state = (state_mlstm_C[:, 0], state_mlstm_n[:, 0][:, :, :, None, :], state_mlstm_m[:, 0][:, :, :, None, None])
    (h2,) = _mlstm(qk, proj1, gate_args, mconsts, h2, DEC_BATCH, DEC_SEQ, N_P, state=state)
    y1 = _mlstm_combine(h2, proj1, mlstm_norm_g[0].reshape(1, D_C))

    y = _out_proj(y1, y1, 0, 1, w_out_odd[0].astype(BF16), x1, mod_rows[1], final_norm_g.reshape(1, D_MODEL))

    y_prompt = y[:N_P].reshape(BATCH, SEQ, D_MODEL).astype(dt)
    y_sample = y[N_P:].reshape(DEC_BATCH, DEC_SEQ, D_MODEL).astype(dt)
    new_mlstm_C = cfin[:, None].astype(dt)
    new_mlstm_n = nfin[:, :, :, 0][:, None].astype(dt)
    new_mlstm_m = mfin[:, :, :, 0, 0][:, None].astype(dt)
    return (y_prompt, y_sample, new_hgrn, new_rwkv, new_mlstm_C, new_mlstm_n, new_mlstm_m)
```
